```python
import jax, jax.numpy as jnp
from jax import lax
import numpy as np

D_MODEL = 1024
BATCH = 8
SEQ = 2048
DEPTH = 1
DEC_BATCH = 128
DEC_SEQ = 1
PAST_LEN = 8192
PAGE_SIZE = 128

HEAD_DIM = 64
MIX_WIDTH = D_MODEL
RWKV_WIDTH = MIX_WIDTH // 2
FOX_WIDTH = MIX_WIDTH - RWKV_WIDTH
RWKV_HEADS = RWKV_WIDTH // HEAD_DIM
FOX_HEADS = FOX_WIDTH // HEAD_DIM
LORA_W = 64
LORA_A = 64
LORA_G = 128
RWKV_COLS = 3 * RWKV_WIDTH + LORA_W + LORA_A + LORA_G
FOX_COLS = 3 * FOX_WIDTH + FOX_HEADS
IN_COLS = RWKV_COLS + FOX_COLS
Q_BLOCK = 128
PEER_HEADS = 8
PEER_DK = 256
N_KEYS = 128
N_EXPERTS = N_KEYS * N_KEYS
PEER_TOPK = 16
PEER_BLOCK = 128
RMS_EPS = 1e-6
GN_EPS = 64e-5

kernel_name = 'hybrid_rwkv7_fox_peer_adaln_step'


def rmsnorm(x, g):
    xf = x.astype(jnp.float32)
    y = xf * lax.rsqrt(jnp.mean(xf * xf, axis=-1, keepdims=True) + RMS_EPS)
    return (y * g.astype(jnp.float32)).astype(x.dtype)


def adaln(c, w_ada, b_ada):
    mod = (jax.nn.silu(c) @ w_ada + b_ada)[:, None, :]
    return jnp.split(mod, 6, axis=-1)


def rwkv7_mixer(p, shift_prev, s0, lp):
    b, t, _ = p.shape
    p_prev = jnp.concatenate([shift_prev[:, None, :], p[:, :-1]], axis=1)
    xs = p + lp['rwkv_mu'] * (p_prev - p)
    o1, o2, o3 = RWKV_WIDTH, 2 * RWKV_WIDTH, 3 * RWKV_WIDTH
    o4 = o3 + LORA_W
    o5 = o4 + LORA_A
    r, k, v = xs[..., :o1], xs[..., o1:o2], xs[..., o2:o3]
    xw, xa, xg = xs[..., o3:o4], xs[..., o4:o5], xs[..., o5:]
    w_log = -jax.nn.softplus(-(lp['rwkv_w0'] + jnp.tanh(xw) @ lp['rwkv_w2'])) - 0.5
    decay = jnp.exp(-jnp.exp(w_log))
    a = jax.nn.sigmoid(lp['rwkv_a0'] + xa @ lp['rwkv_a2'])
    g = jax.nn.sigmoid(xg) @ lp['rwkv_g2']

    def heads(z):
        return z.reshape(b, t, RWKV_HEADS, HEAD_DIM)

    kk = heads(k * lp['rwkv_kk']).astype(jnp.float32)
    kk = (kk / jnp.maximum(jnp.linalg.norm(kk, axis=-1, keepdims=True), 1e-12)).astype(p.dtype)
    k = k * (1 + (a - 1) * lp['rwkv_ka'])

    def step(S, inp):
        r_t, w_t, k_t, v_t, kk_t, a_t = inp
        sa = jnp.einsum('bhvk,bhk->bhv', S, -kk_t)
        S = (S * w_t[:, :, None, :] + sa[..., None] * (kk_t * a_t)[:, :, None, :]
             + v_t[..., None] * k_t[:, :, None, :])
        return S, jnp.einsum('bhvk,bhk->bhv', S, r_t)

    seq_in = (heads(r), heads(decay), heads(k), heads(v), kk, heads(a))
    s_final, y = lax.scan(step, s0, tuple(jnp.moveaxis(z, 1, 0) for z in seq_in))
    y = jnp.moveaxis(y, 0, 1).astype(jnp.float32)
    mean = jnp.mean(y, axis=-1, keepdims=True)
    var = jnp.mean(jnp.square(y - mean), axis=-1, keepdims=True)
    yn = ((y - mean) * lax.rsqrt(var + GN_EPS)).reshape(b, t, RWKV_WIDTH)
    yn = yn * lp['rwkv_gn_w'].astype(jnp.float32) + lp['rwkv_gn_b'].astype(jnp.float32)
    bonus = jnp.sum(heads(r) * heads(k) * lp['rwkv_rk'], axis=-1, keepdims=True) * heads(v)
    out = (yn.astype(p.dtype) + bonus.reshape(b, t, RWKV_WIDTH)) * g
    return out, s_final, p[:, -1]


def fox_prompt(q, k, v, logf):
    b, s, h, d = q.shape
    nb = s // Q_BLOCK
    scale = HEAD_DIM ** -0.5
    c = jnp.cumsum(logf, axis=1)
    kf, vf = k.astype(jnp.float32), v.astype(jnp.float32)
    c_keys = jnp.transpose(c, (0, 2, 1))
    qb = jnp.moveaxis(q.reshape(b, nb, Q_BLOCK, h, d), 1, 0)
    cb = jnp.moveaxis(c.reshape(b, nb, Q_BLOCK, h), 1, 0)
    key_pos = jnp.arange(s)

    def one_block(args):
        q_i, c_i, blk = args
        logits = jnp.einsum('bqhd,bkhd->bhqk', q_i.astype(jnp.float32), kf) * scale
        logits = logits + jnp.transpose(c_i, (0, 2, 1))[..., None] - c_keys[:, :, None, :]
        q_pos = blk * Q_BLOCK + jnp.arange(Q_BLOCK)
        causal = key_pos[None, :] <= q_pos[:, None]
        probs = jax.nn.softmax(jnp.where(causal, logits, -jnp.inf), axis=-1)
        return jnp.einsum('bhqk,bkhd->bqhd', probs, vf)

    out = lax.map(one_block, (qb, cb, jnp.arange(nb)))
    return jnp.moveaxis(out, 0, 1).reshape(b, s, h, d).astype(v.dtype)


def fox_decode(q, k_new, v_new, logf_new, cache_k, cache_v, cache_logf, page_table, layer):
    bd, t, h, d = q.shape
    n_pages = page_table.shape[1]
    scale = HEAD_DIM ** -0.5
    qf = q.astype(jnp.float32)
    c_q = jnp.transpose(jnp.cumsum(logf_new, axis=1), (0, 2, 1))
    logits = jnp.einsum('bqhd,bkhd->bhqk', qf, k_new.astype(jnp.float32)) * scale
    logits = logits + c_q[..., None] - c_q[:, :, None, :]
    logits = jnp.where(jnp.tril(jnp.ones((t, t), dtype=bool)), logits, -jnp.inf)
    m = jnp.max(logits, axis=-1)
    pr = jnp.exp(logits - m[..., None])
    l = jnp.sum(pr, axis=-1)
    acc = jnp.einsum('bhqk,bkhd->bhqd', pr, v_new.astype(jnp.float32))
    past_logf = cache_logf[layer, page_table].astype(jnp.float32).reshape(bd, n_pages * PAGE_SIZE, h)
    suffix = lax.cumsum(past_logf, axis=1, reverse=True) - past_logf
    suffix_pages = jnp.moveaxis(suffix.reshape(bd, n_pages, PAGE_SIZE, h), 1, 0)

    def page_step(carry, inp):
        m, l, acc = carry
        pidx, suf = inp
        kp = cache_k[layer, pidx].astype(jnp.float32)
        vp = cache_v[layer, pidx].astype(jnp.float32)
        lg = jnp.einsum('bqhd,bkhd->bhqk', qf, kp) * scale
        lg = lg + c_q[..., None] + jnp.transpose(suf, (0, 2, 1))[:, :, None, :]
        m_new = jnp.maximum(m, jnp.max(lg, axis=-1))
        alpha = jnp.exp(m - m_new)
        p_blk = jnp.exp(lg - m_new[..., None])
        l_new = l * alpha + jnp.sum(p_blk, axis=-1)
        acc_new = acc * alpha[..., None] + jnp.einsum('bhqk,bkhd->bhqd', p_blk, vp)
        return (m_new, l_new, acc_new), None

    (m, l, acc), _ = lax.scan(page_step, (m, l, acc), (page_table.T, suffix_pages))
    out = acc / l[..., None]
    return jnp.transpose(out, (0, 2, 1, 3)).astype(v_new.dtype)


def peer(h, wq, subkeys, u, v):
    n = h.shape[0]
    n_blocks = -(-n // PEER_BLOCK)
    hp = jnp.pad(h, ((0, n_blocks * PEER_BLOCK - n), (0, 0))).reshape(n_blocks, PEER_BLOCK, D_MODEL)
    half = PEER_DK // 2

    def one_block(xb):
        q = (xb @ wq).reshape(PEER_BLOCK, PEER_HEADS, PEER_DK)
        s1 = jnp.einsum('thd,hnd->thn', q[..., :half], subkeys[:, 0])
        s2 = jnp.einsum('thd,hnd->thn', q[..., half:], subkeys[:, 1])
        v1, i1 = lax.top_k(s1, PEER_TOPK)
        v2, i2 = lax.top_k(s2, PEER_TOPK)
        cand = (v1[..., :, None] + v2[..., None, :]).reshape(PEER_BLOCK, PEER_HEADS, PEER_TOPK * PEER_TOPK)
        cid = (i1[..., :, None] * N_KEYS + i2[..., None, :]).reshape(PEER_BLOCK, PEER_HEADS, PEER_TOPK * PEER_TOPK)
        top, j = lax.top_k(cand, PEER_TOPK)
        eid = jnp.take_along_axis(cid, j, axis=-1)
        gate = jax.nn.softmax(top.astype(jnp.float32), axis=-1).astype(xb.dtype)
        act = jax.nn.gelu(jnp.einsum('thkd,td->thk', u[eid], xb))
        return jnp.einsum('thk,thkd->td', gate * act, v[eid])

    out = lax.map(one_block, hp)
    return out.reshape(n_blocks * PEER_BLOCK, D_MODEL)[:n]


def hybrid_layer(x, c, fox_fn, shift_prev, s0, lp):
    b, t, _ = x.shape
    sh1, sc1, gt1, sh2, sc2, gt2 = adaln(c, lp['w_ada'], lp['b_ada'])
    h = rmsnorm(x, lp['norm1_g']) * (1 + sc1) + sh1
    proj = h @ lp['w_in']
    p_rwkv, p_fox = proj[..., :RWKV_COLS], proj[..., RWKV_COLS:]
    rwkv_out, s_new, shift_new = rwkv7_mixer(p_rwkv, shift_prev, s0, lp)
    q = p_fox[..., :FOX_WIDTH].reshape(b, t, FOX_HEADS, HEAD_DIM)
    k = p_fox[..., FOX_WIDTH:2 * FOX_WIDTH].reshape(b, t, FOX_HEADS, HEAD_DIM)
    v = p_fox[..., 2 * FOX_WIDTH:3 * FOX_WIDTH].reshape(b, t, FOX_HEADS, HEAD_DIM)
    logf = jax.nn.log_sigmoid((p_fox[..., 3 * FOX_WIDTH:] + lp['fox_bf']).astype(jnp.float32))
    fox_out = fox_fn(q, k, v, logf)
    mixed = jnp.concatenate([rwkv_out, fox_out.reshape(b, t, FOX_WIDTH)], axis=-1) @ lp['w_out']
    x = x + gt1 * mixed
    h2 = rmsnorm(x, lp['norm2_g']) * (1 + sc2) + sh2
    ffn = peer(h2.reshape(b * t, D_MODEL), lp['peer_wq'], lp['peer_subkeys'], lp['peer_u'], lp['peer_v'])
    x = x + gt2 * ffn.reshape(b, t, D_MODEL)
    return x, k, v, logf.astype(x.dtype), s_new, shift_new


def setup_inputs(seed: int = 0) -> dict:
    key = jax.random.key(seed)
    ks = jax.random.split(key, 40)
    f32 = jnp.float32

    def nrm(k, shape, s):
        return s * jax.random.normal(k, shape, f32)

    n_pages = PAST_LEN // PAGE_SIZE
    n_used = DEC_BATCH * n_pages
    n_pool = n_used + max(1, n_used // 4)
    page_table = jax.random.permutation(ks[0], n_pool)[:n_used].reshape(DEC_BATCH, n_pages).astype(jnp.int32)
    D = D_MODEL
    return {
        'x_prompt': nrm(ks[1], (BATCH, SEQ, D), 1.0),
        'x_sample': nrm(ks[2], (DEC_BATCH, DEC_SEQ, D), 1.0),
        'cache_k': nrm(ks[3], (DEPTH, n_pool, PAGE_SIZE, FOX_HEADS, HEAD_DIM), 1.0),
        'cache_v': nrm(ks[4], (DEPTH, n_pool, PAGE_SIZE, FOX_HEADS, HEAD_DIM), 1.0),
        'cache_logf': jax.nn.log_sigmoid(3.0 + nrm(ks[5], (DEPTH, n_pool, PAGE_SIZE, FOX_HEADS), 0.5)),
        'state_rwkv': nrm(ks[6], (DEPTH, DEC_BATCH, RWKV_HEADS, HEAD_DIM, HEAD_DIM), 0.3),
        'state_shift': nrm(ks[7], (DEPTH, DEC_BATCH, RWKV_COLS), 1.0),
        'page_table': page_table,
        'c_prompt': nrm(ks[8], (BATCH, D), 1.0),
        'c_sample': nrm(ks[9], (DEC_BATCH, D), 1.0),
        'w_ada': nrm(ks[10], (DEPTH, D, 6 * D), 0.5 * D ** -0.5),
        'b_ada': nrm(ks[11], (DEPTH, 6 * D), 0.02),
        'norm1_g': 1.0 + nrm(ks[12], (DEPTH, D), 0.02),
        'norm2_g': 1.0 + nrm(ks[13], (DEPTH, D), 0.02),
        'w_in': nrm(ks[14], (DEPTH, D, IN_COLS), D ** -0.5),
        'rwkv_mu': jax.random.uniform(ks[15], (DEPTH, RWKV_COLS), f32),
        'rwkv_w0': -1.0 + nrm(ks[16], (DEPTH, RWKV_WIDTH), 0.5),
        'rwkv_w2': nrm(ks[17], (DEPTH, LORA_W, RWKV_WIDTH), LORA_W ** -0.5),
        'rwkv_a0': nrm(ks[18], (DEPTH, RWKV_WIDTH), 0.1),
        'rwkv_a2': nrm(ks[19], (DEPTH, LORA_A, RWKV_WIDTH), LORA_A ** -0.5),
        'rwkv_g2': nrm(ks[20], (DEPTH, LORA_G, RWKV_WIDTH), LORA_G ** -0.5),
        'rwkv_kk': 0.85 + nrm(ks[21], (DEPTH, RWKV_WIDTH), 0.05),
        'rwkv_ka': 1.0 + nrm(ks[22], (DEPTH, RWKV_WIDTH), 0.05),
        'rwkv_rk': nrm(ks[23], (DEPTH, RWKV_HEADS, HEAD_DIM), 0.1),
        'rwkv_gn_w': 1.0 + nrm(ks[24], (DEPTH, RWKV_WIDTH), 0.02),
        'rwkv_gn_b': nrm(ks[25], (DEPTH, RWKV_WIDTH), 0.02),
        'fox_bf': 3.0 + nrm(ks[26], (DEPTH, FOX_HEADS), 0.5),
        'w_out': nrm(ks[27], (DEPTH, MIX_WIDTH, D), MIX_WIDTH ** -0.5),
        'peer_wq': nrm(ks[28], (DEPTH, D, PEER_HEADS * PEER_DK), D ** -0.5),
        'peer_subkeys': nrm(ks[29], (DEPTH, PEER_HEADS, 2, N_KEYS, PEER_DK // 2), (PEER_DK // 2) ** -0.5),
        'peer_u': nrm(ks[30], (DEPTH, N_EXPERTS, D), D ** -0.5),
        'peer_v': nrm(ks[31], (DEPTH, N_EXPERTS, D), 0.5),
        'normf_g': 1.0 + nrm(ks[32], (D,), 0.02),
    }


def reference(x_prompt, x_sample, cache_k, cache_v, cache_logf, state_rwkv, state_shift, page_table,
              c_prompt, c_sample, w_ada, b_ada, norm1_g, norm2_g, w_in, rwkv_mu, rwkv_w0, rwkv_w2,
              rwkv_a0, rwkv_a2, rwkv_g2, rwkv_kk, rwkv_ka, rwkv_rk, rwkv_gn_w, rwkv_gn_b, fox_bf,
              w_out, peer_wq, peer_subkeys, peer_u, peer_v, normf_g):
    xp, xs = x_prompt, x_sample
    new_p, new_s = [], []
    for layer in range(DEPTH):
        lp = {
            'w_ada': w_ada[layer], 'b_ada': b_ada[layer],
            'norm1_g': norm1_g[layer], 'norm2_g': norm2_g[layer],
            'w_in': w_in[layer], 'rwkv_mu': rwkv_mu[layer],
            'rwkv_w0': rwkv_w0[layer], 'rwkv_w2': rwkv_w2[layer],
            'rwkv_a0': rwkv_a0[layer], 'rwkv_a2': rwkv_a2[layer], 'rwkv_g2': rwkv_g2[layer],
            'rwkv_kk': rwkv_kk[layer], 'rwkv_ka': rwkv_ka[layer], 'rwkv_rk': rwkv_rk[layer],
            'rwkv_gn_w': rwkv_gn_w[layer], 'rwkv_gn_b': rwkv_gn_b[layer],
            'fox_bf': fox_bf[layer], 'w_out': w_out[layer],
            'peer_wq': peer_wq[layer], 'peer_subkeys': peer_subkeys[layer],
            'peer_u': peer_u[layer], 'peer_v': peer_v[layer],
        }
        shift0 = jnp.zeros((xp.shape[0], RWKV_COLS), xp.dtype)
        s0 = jnp.zeros((xp.shape[0], RWKV_HEADS, HEAD_DIM, HEAD_DIM), xp.dtype)
        xp, k_p, v_p, lf_p, st_p, sh_p = hybrid_layer(xp, c_prompt, fox_prompt, shift0, s0, lp)

        def fox_sample(q, k, v, logf, layer=layer):
            return fox_decode(q, k, v, logf, cache_k, cache_v, cache_logf, page_table, layer)

        xs, k_s, v_s, lf_s, st_s, sh_s = hybrid_layer(xs, c_sample, fox_sample, state_shift[layer],
                                                      state_rwkv[layer], lp)
        new_p.append((k_p, v_p, lf_p, st_p, sh_p))
        new_s.append((k_s, v_s, lf_s, st_s, sh_s))

    def stack(lst, i):
        return jnp.stack([e[i] for e in lst], axis=0)

    y_prompt = rmsnorm(xp, normf_g)
    y_sample = rmsnorm(xs, normf_g)
    return (y_prompt, y_sample,
            stack(new_p, 0), stack(new_p, 1), stack(new_p, 2), stack(new_p, 3), stack(new_p, 4),
            stack(new_s, 0), stack(new_s, 1), stack(new_s, 2), stack(new_s, 3), stack(new_s, 4))
```

```python
import functools
import math

import jax
import jax.numpy as jnp
from jax import lax
from jax.experimental import pallas as pl
from jax.experimental.pallas import tpu as pltpu

F32 = jnp.float32
BF16 = jnp.bfloat16

HEAD_DIM = 64
LANES = 128
PAGES_PER_STEP = 8
TOPK = 16
N_KEYS = 128
RMS_EPS = 1e-6
GN_EPS = 64e-5
NEG_INF = float("-inf")
VMEM_LIMIT = 56 * 1024 * 1024

_PAIRS = [(i, j) for i in range(TOPK) for j in range(TOPK) if (i + 1) * (j + 1) <= TOPK]
_N_CAND = len(_PAIRS)
_CAND_ROWS = -(-_N_CAND // 8) * 8


def _params(sem, vmem=VMEM_LIMIT):
    return pltpu.CompilerParams(dimension_semantics=sem, vmem_limit_bytes=vmem)


def _dot(a, b):
    return jnp.dot(a, b, preferred_element_type=F32)


def _dot_nt(a, b):
    return lax.dot_general(a, b, (((1,), (1,)), ((), ())), preferred_element_type=F32)


def _split3(x):
    hi = x.astype(BF16)
    r1 = x - hi.astype(F32)
    mid = r1.astype(BF16)
    lo = (r1 - mid.astype(F32)).astype(BF16)
    return hi, mid, lo


def _dot_exact_rhs(x, m):
    hi, mid, lo = _split3(x)
    return _dot(hi, m) + _dot(mid, m) + _dot(lo, m)


def _softplus(z):
    return jnp.maximum(z, 0.0) + jnp.log1p(jnp.exp(-jnp.abs(z)))


def _sigmoid(z):
    return 1.0 / (1.0 + jnp.exp(-z))


def _rms(x, g):
    ms = jnp.mean(x * x, axis=-1, keepdims=True)
    return x * lax.rsqrt(ms + RMS_EPS) * g


def _ada_kernel(c_ref, w_ref, b_ref, o_ref):
    c = c_ref[...]
    s = c * _sigmoid(c)
    o_ref[...] = _dot(s.astype(BF16), w_ref[...].astype(BF16)) + b_ref[...]


def _adaln(c, w_ada, b_ada):
    n, d = c.shape
    cols = w_ada.shape[1]
    tn = 1024
    return pl.pallas_call(
        _ada_kernel,
        grid=(cols // tn,),
        in_specs=[pl.BlockSpec((n, d), lambda j: (0, 0)),
                  pl.BlockSpec((d, tn), lambda j: (0, j)),
                  pl.BlockSpec((1, tn), lambda j: (0, j))],
        out_specs=pl.BlockSpec((n, tn), lambda j: (0, j)),
        out_shape=jax.ShapeDtypeStruct((n, cols), F32),
        compiler_params=_params(("arbitrary",)),
        name="adaln",
    )(c, w_ada, b_ada.reshape(1, cols))


def _in_kernel(x_ref, g_ref, sc_ref, sh_ref, wr_ref, wqkv_ref, wf_ref, bf_ref,
               p_ref, q_ref, k_ref, v_ref, lf_ref, *, fw):
    h = _rms(x_ref[...], g_ref[...]) * (1.0 + sc_ref[0]) + sh_ref[0]
    hb = h.astype(BF16)
    p_ref[...] = _dot(hb, wr_ref[...])
    qkv = _dot(hb, wqkv_ref[...])
    q_ref[...] = qkv[:, :fw]
    k_ref[...] = qkv[:, fw:2 * fw]
    v_ref[...] = qkv[:, 2 * fw:]
    f = _dot(hb, wf_ref[...])[:, :bf_ref.shape[1]] + bf_ref[...]
    lf_ref[...] = -_softplus(-f)


def _mod_spec(mod, rt, rows_per_group):
    mr, d = mod.shape[1], mod.shape[2]
    tpg = rows_per_group // rt
    return pl.BlockSpec((1, mr, d), lambda i: (i // tpg, 0, 0))


def _in_proj(x, g1, sc, sh, wr, wqkv, wf, bfox, rt, rows_per_group):
    n, d = x.shape
    rc, fw3, nh = wr.shape[1], wqkv.shape[1], bfox.shape[1]
    fw = fw3 // 3
    row = lambda w: pl.BlockSpec((rt, w), lambda i: (i, 0))
    full = lambda a: pl.BlockSpec(a.shape, lambda i: (0,) * a.ndim)
    return pl.pallas_call(
        functools.partial(_in_kernel, fw=fw),
        grid=(n // rt,),
        in_specs=[row(d), full(g1), _mod_spec(sc, rt, rows_per_group), _mod_spec(sh, rt, rows_per_group),
                  full(wr), full(wqkv), full(wf), full(bfox)],
        out_specs=[row(rc), row(fw), row(fw), row(fw), row(nh)],
        out_shape=[jax.ShapeDtypeStruct((n, rc), F32), jax.ShapeDtypeStruct((n, fw), F32),
                   jax.ShapeDtypeStruct((n, fw), F32), jax.ShapeDtypeStruct((n, fw), F32),
                   jax.ShapeDtypeStruct((n, nh), F32)],
        compiler_params=_params(("arbitrary",)),
        name="in_proj",
    )(x, g1, sc, sh, wr, wqkv, wf, bfox)


def _prep_kernel(p_ref, pp_ref, mu_ref, w0_ref, a0_ref, kkp_ref, ka_ref, rk_ref,
                 w2_ref, a2_ref, g2_ref, seg_ref,
                 r_ref, w_ref, k_ref, v_ref, nkk_ref, kka_ref, g_ref, bonus_ref, *, rw):
    p = p_ref[...]
    xs = p + mu_ref[...] * (pp_ref[...] - p)
    r = xs[:, :rw]
    k = xs[:, rw:2 * rw]
    v = xs[:, 2 * rw:3 * rw]
    xwa = xs[:, 3 * rw:3 * rw + LANES]
    xg = xs[:, 3 * rw + LANES:]
    seg = seg_ref[...]
    lw = w0_ref[...] + _dot(jnp.tanh(xwa).astype(BF16), w2_ref[...])
    w_log = -_softplus(-lw) - 0.5
    decay = jnp.exp(-jnp.exp(w_log))
    a = _sigmoid(a0_ref[...] + _dot(xwa.astype(BF16), a2_ref[...]))
    g = _dot(_sigmoid(xg).astype(BF16), g2_ref[...])
    kk = k * kkp_ref[...]
    nrm = jnp.sqrt(_dot_exact_rhs(kk * kk, seg))
    kkn = kk / jnp.maximum(nrm, 1e-12)
    k2 = k * (1.0 + (a - 1.0) * ka_ref[...])
    bonus = _dot_exact_rhs(r * k2 * rk_ref[...], seg) * v
    r_ref[...] = r
    w_ref[...] = decay
    k_ref[...] = k2
    v_ref[...] = v
    nkk_ref[...] = -kkn
    kka_ref[...] = kkn * a
    g_ref[...] = g
    bonus_ref[...] = bonus


def _rwkv_prep(p, pprev, vecs, w2p, a2p, g2, seg, rt):
    n, rc = p.shape
    rw = seg.shape[0]
    row = lambda w: pl.BlockSpec((rt, w), lambda i: (i, 0))
    full = lambda a: pl.BlockSpec(a.shape, lambda i: (0,) * a.ndim)
    return pl.pallas_call(
        functools.partial(_prep_kernel, rw=rw),
        grid=(n // rt,),
        in_specs=[row(rc), row(rc)] + [full(a) for a in vecs] + [full(w2p), full(a2p), full(g2), full(seg)],
        out_specs=[row(rw)] * 8,
        out_shape=[jax.ShapeDtypeStruct((n, rw), F32)] * 8,
        compiler_params=_params(("arbitrary",)),
        name="rwkv_prep",
    )(p, pprev, *vecs, w2p, a2p, g2, seg)


def _scan_kernel(r_ref, w_ref, k_ref, nkk_ref, kka_ref, v_ref, s0_ref, y_ref, sT_ref, S, *, tc, nk2):
    tb = pl.program_id(1)

    @pl.when(tb == 0)
    def _():
        S[...] = s0_ref[0]

    def step(t, carry):
        vv = v_ref[0, t]
        sa = jnp.zeros(vv.shape, F32)
        for k2 in range(nk2):
            sa = sa + S[k2] * nkk_ref[0, t, k2:k2 + 1, :]
        sa = sa + pltpu.roll(sa, 64, axis=1)
        y = jnp.zeros(vv.shape, F32)
        for k2 in range(nk2):
            s = (S[k2] * w_ref[0, t, k2:k2 + 1, :] + sa * kka_ref[0, t, k2:k2 + 1, :]
                 + vv * k_ref[0, t, k2:k2 + 1, :])
            S[k2] = s
            y = y + s * r_ref[0, t, k2:k2 + 1, :]
        y = y + pltpu.roll(y, 64, axis=1)
        y_ref[0, t] = y[:, :64]
        return carry

    lax.fori_loop(0, tc, step, 0)

    @pl.when(tb == pl.num_programs(1) - 1)
    def _():
        sT_ref[0] = S[...]


def _rwkv_scan(r, w, k, nkk, kka, v2, s0, tc):
    g, t, nk2, ln = r.shape
    nv = v2.shape[2]
    rowspec = pl.BlockSpec((1, tc, nk2, ln), lambda gi, ti: (gi, ti, 0, 0))
    sspec = pl.BlockSpec((1, nk2, nv, ln), lambda gi, ti: (gi, 0, 0, 0))
    return pl.pallas_call(
        functools.partial(_scan_kernel, tc=tc, nk2=nk2),
        grid=(g, t // tc),
        in_specs=[rowspec] * 5 + [pl.BlockSpec((1, tc, nv, ln), lambda gi, ti: (gi, ti, 0, 0)), sspec],
        out_specs=[pl.BlockSpec((1, tc, nv, 64), lambda gi, ti: (gi, ti, 0, 0)), sspec],
        out_shape=[jax.ShapeDtypeStruct((g, t, nv, 64), F32), jax.ShapeDtypeStruct(s0.shape, F32)],
        scratch_shapes=[pltpu.VMEM((nk2, nv, ln), F32)],
        compiler_params=_params(("arbitrary", "arbitrary")),
        name="rwkv_scan",
    )(r, w, k, nkk, kka, v2, s0)


def _to_scan_rows(a, groups, t):
    a = a.reshape(groups, 8, t, 8, 64).transpose(0, 2, 4, 1, 3)
    return a.reshape(groups, t, 32, 128)


def _to_scan_cols(a, groups, t):
    a = a.reshape(groups, 8, t, 8, 64).transpose(0, 2, 4, 1, 3).reshape(groups, t, 64, 64)
    return jnp.concatenate([a, a], axis=-1)


def _state_to_scan(s, groups):
    s = s.reshape(groups, 8, 8, 64, 32, 2).transpose(0, 4, 3, 5, 1, 2)
    return s.reshape(groups, 32, 64, 128)


def _state_from_scan(s, groups):
    s = s.reshape(groups, 32, 64, 2, 8, 8).transpose(0, 4, 5, 2, 1, 3)
    return s.reshape(groups * 8, 8, 64, 64)


def _y_from_scan(y, groups, t):
    y = y.reshape(groups, t, 64, 8, 8).transpose(0, 3, 1, 4, 2)
    return y.reshape(groups * 8 * t, 8 * 64)


def _cum_kernel(lf_ref, c_ref, *, blk):
    t = lf_ref.shape[2]
    src = lax.broadcasted_iota(jnp.int32, (blk, blk), 0)
    dst = lax.broadcasted_iota(jnp.int32, (blk, blk), 1)
    tri = (src <= dst).astype(BF16)
    carry = jnp.zeros((lf_ref.shape[1], 1), F32)
    for i in range(t // blk):
        cs = _dot_exact_rhs(lf_ref[0, :, i * blk:(i + 1) * blk], tri) + carry
        c_ref[0, :, i * blk:(i + 1) * blk] = cs
        carry = cs[:, blk - 1:blk]


def _cumsum_t(lf_t):
    b, h, t = lf_t.shape
    blk = min(256, t)
    spec = pl.BlockSpec((1, h, t), lambda i: (i, 0, 0))
    return pl.pallas_call(
        functools.partial(_cum_kernel, blk=blk),
        grid=(b,), in_specs=[spec], out_specs=spec,
        out_shape=jax.ShapeDtypeStruct((b, h, t), F32),
        compiler_params=_params(("arbitrary",)),
        name="logf_cumsum",
    )(lf_t)


def _attn_kernel(q_ref, k_ref, v_ref, cc_ref, cr_ref, o_ref, *, tq):
    qi = pl.program_id(2)
    q = (q_ref[0, 0] * (HEAD_DIM ** -0.5)).astype(BF16)
    ci = cc_ref[0, 0]

    def block(j, carry, masked):
        m, l, acc = carry
        start = pl.multiple_of(j * tq, tq)
        kj = k_ref[0, 0, pl.ds(start, tq), :].astype(BF16)
        vj = v_ref[0, 0, pl.ds(start, tq), :].astype(BF16)
        s = _dot_nt(q, kj) + ci - cr_ref[0, 0, j]
        if masked:
            row = lax.broadcasted_iota(jnp.int32, (tq, tq), 0)
            col = lax.broadcasted_iota(jnp.int32, (tq, tq), 1)
            s = jnp.where(col <= row, s, NEG_INF)
        m_new = jnp.maximum(m, jnp.max(s, axis=-1, keepdims=True))
        alpha = jnp.exp(m - m_new)
        p = jnp.exp(s - m_new)
        l = l * alpha + jnp.sum(p, axis=-1, keepdims=True)
        acc = acc * alpha + _dot(p.astype(BF16), vj)
        return m_new, l, acc

    init = (jnp.full((tq, 1), NEG_INF, F32), jnp.zeros((tq, 1), F32), jnp.zeros((tq, HEAD_DIM), F32))
    carry = lax.fori_loop(0, qi, lambda j, c: block(j, c, False), init)
    m, l, acc = block(qi, carry, True)
    o_ref[0, 0] = acc / l


def _fox_prompt(q, k, v, c_t, tq):
    b, h, t, d = q.shape
    nq = t // tq
    c_col = c_t[..., None]
    c_row = c_t.reshape(b, h, nq, 1, tq)
    qspec = pl.BlockSpec((1, 1, tq, d), lambda bi, hi, i: (bi, hi, i, 0))
    kvspec = pl.BlockSpec((1, 1, t, d), lambda bi, hi, i: (bi, hi, 0, 0))
    return pl.pallas_call(
        functools.partial(_attn_kernel, tq=tq),
        grid=(b, h, nq),
        in_specs=[qspec, kvspec, kvspec,
                  pl.BlockSpec((1, 1, tq, 1), lambda bi, hi, i: (bi, hi, i, 0)),
                  pl.BlockSpec((1, 1, nq, 1, tq), lambda bi, hi, i: (bi, hi, 0, 0, 0))],
        out_specs=qspec,
        out_shape=jax.ShapeDtypeStruct((b, h, t, d), F32),
        compiler_params=_params(("arbitrary", "arbitrary", "arbitrary")),
        name="fox_prompt",
    )(q, k, v, c_col, c_row)


def _dec_kernel(pt_ref, q_ref, kn_ref, vn_ref, lfn_ref, *refs, npg):
    k_refs = refs[:npg]
    v_refs = refs[npg:2 * npg]
    lf_refs = refs[2 * npg:3 * npg]
    o_ref = refs[3 * npg]
    m_s, l_s, acc_s, car_s = refs[3 * npg + 1:]
    s_id = pl.program_id(1)
    nh = m_s.shape[0]
    width = q_ref.shape[2]
    page = k_refs[0].shape[1]
    head_of_col = lax.broadcasted_iota(jnp.int32, (nh, width), 1) // HEAD_DIM
    diag = head_of_col == lax.broadcasted_iota(jnp.int32, (nh, width), 0)
    qbd = jnp.where(diag, q_ref[0] * (HEAD_DIM ** -0.5), 0.0)

    @pl.when(s_id == 0)
    def _():
        m_s[...] = jnp.sum(qbd * kn_ref[0], axis=1, keepdims=True)
        l_s[...] = jnp.ones(l_s.shape, F32)
        acc_s[...] = jnp.broadcast_to(vn_ref[0], acc_s.shape)
        car_s[...] = jnp.zeros(car_s.shape, F32)

    src = lax.broadcasted_iota(jnp.int32, (page, page), 0)
    dst = lax.broadcasted_iota(jnp.int32, (page, page), 1)
    later = (src > dst).astype(BF16)
    qb = qbd.astype(BF16)
    carry = car_s[...]
    base = lfn_ref[0]
    logits = []
    for i in range(npg):
        lf = lf_refs[i][0]
        bias = _dot_exact_rhs(lf, later) + carry + base
        carry = carry + jnp.sum(lf, axis=1, keepdims=True)
        logits.append(_dot_nt(qb, k_refs[i][0].astype(BF16)) + bias)
    car_s[...] = carry
    s = jnp.concatenate(logits, axis=1)
    m_old = m_s[...]
    m_new = jnp.maximum(m_old, jnp.max(s, axis=1, keepdims=True))
    alpha = jnp.exp(m_old - m_new)
    p = jnp.exp(s - m_new)
    l_s[...] = l_s[...] * alpha + jnp.sum(p, axis=1, keepdims=True)
    m_s[...] = m_new
    pb = p.astype(BF16)
    acc = acc_s[...] * alpha
    for i in range(npg):
        acc = acc + _dot(pb[:, i * page:(i + 1) * page], v_refs[i][0].astype(BF16))
    acc_s[...] = acc

    @pl.when(s_id == pl.num_programs(1) - 1)
    def _():
        o = jnp.where(diag, acc / l_s[...], 0.0)
        o_ref[0] = jnp.sum(o, axis=0, keepdims=True)


def _fox_decode(q, k_new, v_new, lf_new, cache_k, cache_v, cache_lf_t, page_table):
    bd, width = q.shape
    nh = lf_new.shape[1]
    n_pages = page_table.shape[1]
    page = cache_k.shape[1]
    npg = PAGES_PER_STEP
    assert n_pages % npg == 0
    steps = n_pages // npg

    def page_idx(i):
        return lambda b, s, pt: (pt[b, n_pages - 1 - (s * npg + i)], 0, 0)

    rowspec = pl.BlockSpec((1, 1, width), lambda b, s, pt: (b, 0, 0))
    in_specs = [rowspec, rowspec, rowspec, pl.BlockSpec((1, nh, 1), lambda b, s, pt: (b, 0, 0))]
    in_specs += [pl.BlockSpec((1, page, width), page_idx(i)) for i in range(npg)]
    in_specs += [pl.BlockSpec((1, page, width), page_idx(i)) for i in range(npg)]
    in_specs += [pl.BlockSpec((1, nh, page), page_idx(i)) for i in range(npg)]
    grid_spec = pltpu.PrefetchScalarGridSpec(
        num_scalar_prefetch=1, grid=(bd, steps), in_specs=in_specs, out_specs=rowspec,
        scratch_shapes=[pltpu.VMEM((nh, 1), F32), pltpu.VMEM((nh, 1), F32),
                        pltpu.VMEM((nh, width), F32), pltpu.VMEM((nh, 1), F32)])
    out = pl.pallas_call(
        functools.partial(_dec_kernel, npg=npg),
        grid_spec=grid_spec,
        out_shape=jax.ShapeDtypeStruct((bd, 1, width), F32),
        compiler_params=_params(("arbitrary", "arbitrary")),
        name="fox_decode",
    )(page_table, q[:, None, :], k_new[:, None, :], v_new[:, None, :], lf_new[:, :, None],
      *([cache_k] * npg), *([cache_v] * npg), *([cache_lf_t] * npg))
    return out.reshape(bd, width)


def _out_kernel(y_ref, bonus_ref, g_ref, fox_ref, x_ref, gt_ref, sc_ref, sh_ref,
                gnw_ref, gnb_ref, seg_ref, wo_ref, g2_ref, x1_ref, h2_ref, h2b_ref, *, rw):
    seg = seg_ref[...]
    y = y_ref[...]
    mean = _dot_exact_rhs(y, seg) * (1.0 / HEAD_DIM)
    d = y - mean
    var = _dot_exact_rhs(d * d, seg) * (1.0 / HEAD_DIM)
    yn = d * lax.rsqrt(var + GN_EPS) * gnw_ref[...] + gnb_ref[...]
    ro = (yn + bonus_ref[...]) * g_ref[...]
    mixed = _dot(ro.astype(BF16), wo_ref[:rw, :]) + _dot(fox_ref[...].astype(BF16), wo_ref[rw:, :])
    x1 = x_ref[...] + gt_ref[0] * mixed
    h2 = _rms(x1, g2_ref[...]) * (1.0 + sc_ref[0]) + sh_ref[0]
    x1_ref[...] = x1
    h2_ref[...] = h2
    h2b_ref[...] = h2.astype(BF16)


def _out_proj(y, bonus, g, fox, x, gt, sc, sh, gnw, gnb, seg, wo, g2, rt, rows_per_group):
    n, d = x.shape
    rw = y.shape[1]
    row = lambda w: pl.BlockSpec((rt, w), lambda i: (i, 0))
    full = lambda a: pl.BlockSpec(a.shape, lambda i: (0,) * a.ndim)
    ms = lambda m: _mod_spec(m, rt, rows_per_group)
    return pl.pallas_call(
        functools.partial(_out_kernel, rw=rw),
        grid=(n // rt,),
        in_specs=[row(rw), row(rw), row(rw), row(fox.shape[1]), row(d), ms(gt), ms(sc), ms(sh),
                  full(gnw), full(gnb), full(seg), full(wo), full(g2)],
        out_specs=[row(d), row(d), row(d)],
        out_shape=[jax.ShapeDtypeStruct((n, d), F32), jax.ShapeDtypeStruct((n, d), F32),
                   jax.ShapeDtypeStruct((n, d), BF16)],
        compiler_params=_params(("arbitrary",)),
        name="out_proj",
    )(y, bonus, g, fox, x, gt, sc, sh, gnw, gnb, seg, wo, g2)


def _top16(s, rows):
    iota = lax.broadcasted_iota(jnp.int32, s.shape, 0).astype(F32)
    rank = jnp.full(s.shape, float(TOPK), F32)
    cur = s
    vals = []
    for i in range(TOPK):
        m = jnp.max(cur, axis=0, keepdims=True)
        idx = jnp.min(jnp.where(cur == m, iota, float(rows)), axis=0, keepdims=True)
        sel = iota == idx
        rank = jnp.where(sel, float(i), rank)
        cur = jnp.where(sel, NEG_INF, cur)
        vals.append(m)
    return vals, rank


def _route_kernel(h_ref, wq_ref, sk_ref, r2_ref, e2_ref, c1_ref, e1_ref, cand, selm, *, nheads):
    rt = h_ref.shape[0]
    q_t = _dot_nt(wq_ref[...], h_ref[...])
    cand[_CAND_ROWS - 8:, :] = jnp.full((8, rt), NEG_INF, F32)
    iota_c = lax.broadcasted_iota(jnp.int32, (_CAND_ROWS, rt), 0).astype(F32)
    for h in range(nheads):
        qa = q_t[(2 * h) * N_KEYS:(2 * h + 1) * N_KEYS].astype(BF16)
        qb = q_t[(2 * h + 1) * N_KEYS:(2 * h + 2) * N_KEYS].astype(BF16)
        s1 = _dot(sk_ref[2 * h], qa)
        s2 = _dot(sk_ref[2 * h + 1], qb)
        v1, rank1 = _top16(s1, N_KEYS)
        v2, rank2 = _top16(s2, N_KEYS)
        for row, (i, j) in enumerate(_PAIRS):
            cand[row:row + 1, :] = v1[i] + v2[j]
        cur = cand[...]
        top = cur[0:1, :]
        sel_all = jnp.zeros(cur.shape, F32)
        for _ in range(TOPK):
            m = jnp.max(cur, axis=0, keepdims=True)
            idx = jnp.min(jnp.where(cur == m, iota_c, float(_CAND_ROWS)), axis=0, keepdims=True)
            sel = iota_c == idx
            sel_all = jnp.where(sel, 1.0, sel_all)
            cur = jnp.where(sel, NEG_INF, cur)
        z = jnp.sum(sel_all * jnp.exp(cand[...] - top), axis=0, keepdims=True)
        selm[...] = sel_all
        c1 = jnp.zeros(rank1.shape, F32)
        row = 0
        for i in range(TOPK):
            n_i = TOPK // (i + 1)
            cnt = jnp.sum(selm[row:row + n_i, :], axis=0, keepdims=True)
            c1 = jnp.where(rank1 == float(i), cnt, c1)
            row += n_i
        e1 = jnp.where(rank1 < float(TOPK), jnp.exp(s1 - v1[0]), 0.0) * (1.0 / z)
        e2 = jnp.where(rank2 < float(TOPK), jnp.exp(s2 - v2[0]), 0.0)
        r2_ref[h] = rank2
        e2_ref[h] = e2
        c1_ref[h] = c1
        e1_ref[h] = e1


def _peer_route(h2b, wq_t, sk, rt):
    n, d = h2b.shape
    nheads = sk.shape[0] // 2
    tab = pl.BlockSpec((nheads, N_KEYS, rt), lambda i: (0, 0, i))
    full = lambda a: pl.BlockSpec(a.shape, lambda i: (0,) * a.ndim)
    return pl.pallas_call(
        functools.partial(_route_kernel, nheads=nheads),
        grid=(n // rt,),
        in_specs=[pl.BlockSpec((rt, d), lambda i: (i, 0)), full(wq_t), full(sk)],
        out_specs=[tab] * 4,
        out_shape=[jax.ShapeDtypeStruct((nheads, N_KEYS, n), F32)] * 4,
        scratch_shapes=[pltpu.VMEM((_CAND_ROWS, rt), F32), pltpu.VMEM((_CAND_ROWS, rt), F32)],
        compiler_params=_params(("arbitrary",)),
        name="peer_route",
    )(h2b, wq_t, sk)


def _gelu_tanh(x):
    return 0.5 * x * (1.0 + jnp.tanh(math.sqrt(2.0 / math.pi) * (x + 0.044715 * (x * x * x))))


def _peer_kernel(h_ref, u_ref, vt_ref, r2_ref, e2_ref, c1_ref, e1_ref, o_ref, p_s, *, a_per, nheads):
    j = pl.program_id(1)

    @pl.when(j == 0)
    def _():
        o_ref[...] = jnp.zeros(o_ref.shape, F32)

    h_t = _dot_nt(u_ref[...], h_ref[...])
    for a in range(a_per):
        a_glob = j * a_per + a
        act = _gelu_tanh(h_t[a * N_KEYS:(a + 1) * N_KEYS])
        gate = None
        for h in range(nheads):
            cnt = c1_ref[h, pl.ds(a_glob, 1), :]
            w1 = e1_ref[h, pl.ds(a_glob, 1), :]
            term = jnp.where(r2_ref[h] < cnt, e2_ref[h], 0.0) * w1
            gate = term if gate is None else gate + term
        p_s[a * N_KEYS:(a + 1) * N_KEYS, :] = (gate * act).astype(BF16)
    o_ref[...] += _dot(vt_ref[...], p_s[...])


def _peer_dense(h2b, u_b, v_t, r2, e2, c1, e1, rt, a_per):
    n, d = h2b.shape
    n_exp = u_b.shape[0]
    nheads = r2.shape[0]
    et = a_per * N_KEYS
    tab = pl.BlockSpec((nheads, N_KEYS, rt), lambda i, j: (0, 0, i))
    return pl.pallas_call(
        functools.partial(_peer_kernel, a_per=a_per, nheads=nheads),
        grid=(n // rt, n_exp // et),
        in_specs=[pl.BlockSpec((rt, d), lambda i, j: (i, 0)),
                  pl.BlockSpec((et, d), lambda i, j: (j, 0)),
                  pl.BlockSpec((d, et), lambda i, j: (0, j)),
                  tab, tab, tab, tab],
        out_specs=pl.BlockSpec((d, rt), lambda i, j: (0, i)),
        out_shape=jax.ShapeDtypeStruct((d, n), F32),
        scratch_shapes=[pltpu.VMEM((et, rt), BF16)],
        compiler_params=_params(("arbitrary", "arbitrary")),
        name="peer_dense",
    )(h2b, u_b, v_t, r2, e2, c1, e1)


def _fin_kernel(x_ref, f_ref, gt_ref, g_ref, o_ref):
    o_ref[...] = _rms(x_ref[...] + gt_ref[0] * f_ref[...], g_ref[...])


def _final(x1, ffn, gt, gf, rt, rows_per_group):
    n, d = x1.shape
    row = pl.BlockSpec((rt, d), lambda i: (i, 0))
    return pl.pallas_call(
        _fin_kernel,
        grid=(n // rt,),
        in_specs=[row, row, _mod_spec(gt, rt, rows_per_group), pl.BlockSpec(gf.shape, lambda i: (0, 0))],
        out_specs=row,
        out_shape=jax.ShapeDtypeStruct((n, d), F32),
        compiler_params=_params(("arbitrary",)),
        name="final_norm",
    )(x1, ffn, gt, gf)


def _layer(x, mods, per_row, fox_fn, shift_prev, s0, lw, t, rt, peer_rt):
    n, d = x.shape
    nb = n // t
    rows_per_group = n if per_row else t
    sh1, sc1, gt1, sh2, sc2, gt2 = mods
    p, q, k, v, lf = _in_proj(x, lw["g1"], sc1, sh1, lw["wr"], lw["wqkv"], lw["wf"], lw["bf"], rt,
                              rows_per_group)
    rc = p.shape[1]
    p3 = p.reshape(nb, t, rc)
    pprev = jnp.concatenate([shift_prev[:, None, :], p3[:, :-1]], axis=1).reshape(n, rc)
    r, w, k2, vv, nkk, kka, g, bonus = _rwkv_prep(p, pprev, lw["vecs"], lw["w2p"], lw["a2p"], lw["g2"],
                                                  lw["seg"], rt)
    groups = nb // 8
    tc = min(64, t)
    rows = [_to_scan_rows(a, groups, t) for a in (r, w, k2, nkk, kka)]
    y, s_new = _rwkv_scan(*rows, _to_scan_cols(vv, groups, t), _state_to_scan(s0, groups), tc)
    y = _y_from_scan(y, groups, t)
    s_new = _state_from_scan(s_new, groups)
    fox = fox_fn(q, k, v, lf)
    x1, h2, h2b = _out_proj(y, bonus, g, fox, x, gt1, sc2, sh2, lw["gnw"], lw["gnb"], lw["seg"], lw["wo"],
                            lw["g2n"], rt, rows_per_group)
    del h2
    r2, e2, c1, e1 = _peer_route(h2b, lw["wq_t"], lw["sk"], peer_rt)
    ffn_t = _peer_dense(h2b, lw["u_b"], lw["v_t"], r2, e2, c1, e1, peer_rt, 4)
    return x1, ffn_t.T, gt2, k, v, lf, s_new, p3[:, -1]


def kernel(x_prompt, x_sample, cache_k, cache_v, cache_logf, state_rwkv, state_shift, page_table, c_prompt, c_sample, w_ada, b_ada, norm1_g, norm2_g, w_in, rwkv_mu, rwkv_w0, rwkv_w2, rwkv_a0, rwkv_a2, rwkv_g2, rwkv_kk, rwkv_ka, rwkv_rk, rwkv_gn_w, rwkv_gn_b, fox_bf, w_out, peer_wq, peer_subkeys, peer_u, peer_v, normf_g):
    depth = w_in.shape[0]
    b, t, d = x_prompt.shape
    bd, td, _ = x_sample.shape
    assert td == 1 and depth == 1
    rw = rwkv_w0.shape[1]
    lora_w, lora_a = rwkv_w2.shape[1], rwkv_a2.shape[1]
    assert lora_w + lora_a == LANES
    rc = rwkv_mu.shape[1]
    nh = fox_bf.shape[1]
    fw = nh * HEAD_DIM
    n_pool, page = cache_k.shape[1], cache_k.shape[2]

    xp = x_prompt.reshape(b * t, d)
    xs = x_sample.reshape(bd, d)
    seg = (jnp.arange(rw)[:, None] // HEAD_DIM == jnp.arange(rw)[None, :] // HEAD_DIM).astype(BF16)
    new_p, new_s = [], []
    for layer in range(depth):
        wl = w_in[layer]
        lw = {
            "g1": norm1_g[layer].reshape(1, d), "g2n": norm2_g[layer].reshape(1, d),
            "wr": wl[:, :rc].astype(BF16), "wqkv": wl[:, rc:rc + 3 * fw].astype(BF16),
            "wf": jnp.pad(wl[:, rc + 3 * fw:], ((0, 0), (0, LANES - nh))).astype(BF16),
            "bf": fox_bf[layer].reshape(1, nh),
            "vecs": [rwkv_mu[layer].reshape(1, rc), rwkv_w0[layer].reshape(1, rw), rwkv_a0[layer].reshape(1, rw),
                     rwkv_kk[layer].reshape(1, rw), rwkv_ka[layer].reshape(1, rw), rwkv_rk[layer].reshape(1, rw)],
            "w2p": jnp.pad(rwkv_w2[layer], ((0, lora_a), (0, 0))).astype(BF16),
            "a2p": jnp.pad(rwkv_a2[layer], ((lora_w, 0), (0, 0))).astype(BF16),
            "g2": rwkv_g2[layer].astype(BF16), "seg": seg,
            "gnw": rwkv_gn_w[layer].reshape(1, rw), "gnb": rwkv_gn_b[layer].reshape(1, rw),
            "wo": w_out[layer].astype(BF16),
            "wq_t": peer_wq[layer].T.astype(BF16),
            "sk": peer_subkeys[layer].reshape(-1, N_KEYS, peer_subkeys.shape[-1]).astype(BF16),
            "u_b": peer_u[layer].astype(BF16), "v_t": peer_v[layer].T.astype(BF16),
        }
        mod = _adaln(jnp.concatenate([c_prompt, c_sample], axis=0), w_ada[layer], b_ada[layer])
        mods_p = [mod[:b, i * d:(i + 1) * d].reshape(b, 1, d) for i in range(6)]
        mods_s = [mod[b:, i * d:(i + 1) * d].reshape(1, bd, d) for i in range(6)]

        def fox_p(q, k, v, lf):
            heads = lambda a: a.reshape(b, t, nh, HEAD_DIM).transpose(0, 2, 1, 3)
            c_t = _cumsum_t(lf.reshape(b, t, nh).transpose(0, 2, 1))
            o = _fox_prompt(heads(q), heads(k), heads(v), c_t, min(256, t))
            return o.transpose(0, 2, 1, 3).reshape(b * t, fw)

        def fox_s(q, k, v, lf, layer=layer):
            return _fox_decode(q, k, v, lf, cache_k[layer].reshape(n_pool, page, fw),
                               cache_v[layer].reshape(n_pool, page, fw),
                               cache_logf[layer].transpose(0, 2, 1), page_table)

        rt_p = min(512, t)
        x1p, ffn_p, gt2p, k_p, v_p, lf_p, st_p, sh_p = _layer(
            xp, mods_p, False, fox_p, jnp.zeros((b, rc), F32), jnp.zeros((b, rw // HEAD_DIM, HEAD_DIM, HEAD_DIM), F32),
            lw, t, rt_p, min(256, t))
        x1s, ffn_s, gt2s, k_s, v_s, lf_s, st_s, sh_s = _layer(
            xs, mods_s, True, fox_s, state_shift[layer], state_rwkv[layer], lw, 1, bd, bd)
        gf = normf_g.reshape(1, d)
        xp = _final(x1p, ffn_p, gt2p, gf, rt_p, t)
        xs = _final(x1s, ffn_s, gt2s, gf, bd, bd)
        new_p.append((k_p.reshape(b, t, nh, HEAD_DIM), v_p.reshape(b, t, nh, HEAD_DIM), lf_p.reshape(b, t, nh),
                      st_p, sh_p))
        new_s.append((k_s.reshape(bd, 1, nh, HEAD_DIM), v_s.reshape(bd, 1, nh, HEAD_DIM), lf_s.reshape(bd, 1, nh),
                      st_s, sh_s))

    stack = lambda lst, i: jnp.stack([e[i] for e in lst], axis=0)
    return (xp.reshape(b, t, d), xs.reshape(bd, 1, d),
            stack(new_p, 0), stack(new_p, 1), stack(new_p, 2), stack(new_p, 3), stack(new_p, 4),
            stack(new_s, 0), stack(new_s, 1), stack(new_s, 2), stack(new_s, 3), stack(new_s, 4))
```

```python
import functools
import math

import jax
import jax.numpy as jnp
from jax import lax
from jax.experimental import pallas as pl
from jax.experimental.pallas import tpu as pltpu

F32 = jnp.float32
BF16 = jnp.bfloat16

HEAD_DIM = 64
LANES = 128
PAGES_PER_STEP = 8
PEER_A_PER_STEP = 16
TOPK = 16
N_KEYS = 128
RMS_EPS = 1e-6
GN_EPS = 64e-5
NEG_INF = float("-inf")
VMEM_LIMIT = 56 * 1024 * 1024

_PAIRS = [(i, j) for i in range(TOPK) for j in range(TOPK) if (i + 1) * (j + 1) <= TOPK]
_N_CAND = len(_PAIRS)
_CAND_ROWS = -(-_N_CAND // 8) * 8


def _params(sem, vmem=VMEM_LIMIT):
    return pltpu.CompilerParams(dimension_semantics=sem, vmem_limit_bytes=vmem)


def _dot(a, b):
    return jnp.dot(a, b, preferred_element_type=F32)


def _dot_nt(a, b):
    return lax.dot_general(a, b, (((1,), (1,)), ((), ())), preferred_element_type=F32)


def _split3(x):
    hi = x.astype(BF16)
    r1 = x - hi.astype(F32)
    mid = r1.astype(BF16)
    lo = (r1 - mid.astype(F32)).astype(BF16)
    return hi, mid, lo


def _dot_exact_rhs(x, m):
    hi, mid, lo = _split3(x)
    return _dot(hi, m) + _dot(mid, m) + _dot(lo, m)


def _softplus(z):
    return jnp.maximum(z, 0.0) + jnp.log1p(jnp.exp(-jnp.abs(z)))


def _sigmoid(z):
    return 1.0 / (1.0 + jnp.exp(-z))


def _rms(x, g):
    ms = jnp.mean(x * x, axis=-1, keepdims=True)
    return x * lax.rsqrt(ms + RMS_EPS) * g


def _ada_kernel(c_ref, w_ref, b_ref, o_ref):
    c = c_ref[...]
    s = c * _sigmoid(c)
    o_ref[...] = _dot(s.astype(BF16), w_ref[...].astype(BF16)) + b_ref[...]


def _adaln(c, w_ada, b_ada):
    n, d = c.shape
    cols = w_ada.shape[1]
    tn = 1024
    return pl.pallas_call(
        _ada_kernel,
        grid=(cols // tn,),
        in_specs=[pl.BlockSpec((n, d), lambda j: (0, 0)),
                  pl.BlockSpec((d, tn), lambda j: (0, j)),
                  pl.BlockSpec((1, tn), lambda j: (0, j))],
        out_specs=pl.BlockSpec((n, tn), lambda j: (0, j)),
        out_shape=jax.ShapeDtypeStruct((n, cols), F32),
        compiler_params=_params(("arbitrary",)),
        name="adaln",
    )(c, w_ada, b_ada.reshape(1, cols))


def _in_kernel(x_ref, g_ref, sc_ref, sh_ref, wr_ref, wqkv_ref, wf_ref, bf_ref,
               p_ref, q_ref, k_ref, v_ref, lf_ref, *, fw):
    h = _rms(x_ref[...], g_ref[...]) * (1.0 + sc_ref[0]) + sh_ref[0]
    hb = h.astype(BF16)
    p_ref[...] = _dot(hb, wr_ref[...])
    qkv = _dot(hb, wqkv_ref[...])
    q_ref[...] = qkv[:, :fw]
    k_ref[...] = qkv[:, fw:2 * fw]
    v_ref[...] = qkv[:, 2 * fw:]
    f = _dot(hb, wf_ref[...])[:, :bf_ref.shape[1]] + bf_ref[...]
    lf_ref[...] = -_softplus(-f)


def _mod_spec(mod, rt, rows_per_group):
    mr, d = mod.shape[1], mod.shape[2]
    tpg = rows_per_group // rt
    return pl.BlockSpec((1, mr, d), lambda i: (i // tpg, 0, 0))


def _in_proj(x, g1, sc, sh, wr, wqkv, wf, bfox, rt, rows_per_group):
    n, d = x.shape
    rc, fw3, nh = wr.shape[1], wqkv.shape[1], bfox.shape[1]
    fw = fw3 // 3
    row = lambda w: pl.BlockSpec((rt, w), lambda i: (i, 0))
    full = lambda a: pl.BlockSpec(a.shape, lambda i: (0,) * a.ndim)
    return pl.pallas_call(
        functools.partial(_in_kernel, fw=fw),
        grid=(n // rt,),
        in_specs=[row(d), full(g1), _mod_spec(sc, rt, rows_per_group), _mod_spec(sh, rt, rows_per_group),
                  full(wr), full(wqkv), full(wf), full(bfox)],
        out_specs=[row(rc), row(fw), row(fw), row(fw), row(nh)],
        out_shape=[jax.ShapeDtypeStruct((n, rc), F32), jax.ShapeDtypeStruct((n, fw), F32),
                   jax.ShapeDtypeStruct((n, fw), F32), jax.ShapeDtypeStruct((n, fw), F32),
                   jax.ShapeDtypeStruct((n, nh), F32)],
        compiler_params=_params(("arbitrary",)),
        name="in_proj",
    )(x, g1, sc, sh, wr, wqkv, wf, bfox)


def _prep_kernel(p_ref, pp_ref, mu_ref, w0_ref, a0_ref, kkp_ref, ka_ref, rk_ref,
                 w2_ref, a2_ref, g2_ref, seg_ref,
                 r_ref, w_ref, k_ref, v_ref, nkk_ref, kka_ref, g_ref, bonus_ref, *, rw):
    p = p_ref[...]
    xs = p + mu_ref[...] * (pp_ref[...] - p)
    r = xs[:, :rw]
    k = xs[:, rw:2 * rw]
    v = xs[:, 2 * rw:3 * rw]
    xwa = xs[:, 3 * rw:3 * rw + LANES]
    xg = xs[:, 3 * rw + LANES:]
    seg = seg_ref[...]
    lw = w0_ref[...] + _dot(jnp.tanh(xwa).astype(BF16), w2_ref[...])
    w_log = -_softplus(-lw) - 0.5
    decay = jnp.exp(-jnp.exp(w_log))
    a = _sigmoid(a0_ref[...] + _dot(xwa.astype(BF16), a2_ref[...]))
    g = _dot(_sigmoid(xg).astype(BF16), g2_ref[...])
    kk = k * kkp_ref[...]
    nrm = jnp.sqrt(_dot_exact_rhs(kk * kk, seg))
    kkn = kk / jnp.maximum(nrm, 1e-12)
    k2 = k * (1.0 + (a - 1.0) * ka_ref[...])
    bonus = _dot_exact_rhs(r * k2 * rk_ref[...], seg) * v
    r_ref[...] = r
    w_ref[...] = decay
    k_ref[...] = k2
    v_ref[...] = v
    nkk_ref[...] = -kkn
    kka_ref[...] = kkn * a
    g_ref[...] = g
    bonus_ref[...] = bonus


def _rwkv_prep(p, pprev, vecs, w2p, a2p, g2, seg, rt):
    n, rc = p.shape
    rw = seg.shape[0]
    row = lambda w: pl.BlockSpec((rt, w), lambda i: (i, 0))
    full = lambda a: pl.BlockSpec(a.shape, lambda i: (0,) * a.ndim)
    return pl.pallas_call(
        functools.partial(_prep_kernel, rw=rw),
        grid=(n // rt,),
        in_specs=[row(rc), row(rc)] + [full(a) for a in vecs] + [full(w2p), full(a2p), full(g2), full(seg)],
        out_specs=[row(rw)] * 8,
        out_shape=[jax.ShapeDtypeStruct((n, rw), F32)] * 8,
        compiler_params=_params(("arbitrary",)),
        name="rwkv_prep",
    )(p, pprev, *vecs, w2p, a2p, g2, seg)


def _scan_kernel(r_ref, w_ref, k_ref, nkk_ref, kka_ref, v_ref, s0_ref, y_ref, sT_ref, S, *, tc, nk2):
    tb = pl.program_id(1)

    @pl.when(tb == 0)
    def _():
        S[...] = s0_ref[0]

    def step(t, carry):
        vv = v_ref[0, t]
        sa = jnp.zeros(vv.shape, F32)
        for k2 in range(nk2):
            sa = sa + S[k2] * nkk_ref[0, t, k2:k2 + 1, :]
        sa = sa + pltpu.roll(sa, 64, axis=1)
        y = jnp.zeros(vv.shape, F32)
        for k2 in range(nk2):
            s = (S[k2] * w_ref[0, t, k2:k2 + 1, :] + sa * kka_ref[0, t, k2:k2 + 1, :]
                 + vv * k_ref[0, t, k2:k2 + 1, :])
            S[k2] = s
            y = y + s * r_ref[0, t, k2:k2 + 1, :]
        y = y + pltpu.roll(y, 64, axis=1)
        y_ref[0, t] = y[:, :64]
        return carry

    lax.fori_loop(0, tc, step, 0)

    @pl.when(tb == pl.num_programs(1) - 1)
    def _():
        sT_ref[0] = S[...]


def _rwkv_scan(r, w, k, nkk, kka, v2, s0, tc):
    g, t, nk2, ln = r.shape
    nv = v2.shape[2]
    rowspec = pl.BlockSpec((1, tc, nk2, ln), lambda gi, ti: (gi, ti, 0, 0))
    sspec = pl.BlockSpec((1, nk2, nv, ln), lambda gi, ti: (gi, 0, 0, 0))
    return pl.pallas_call(
        functools.partial(_scan_kernel, tc=tc, nk2=nk2),
        grid=(g, t // tc),
        in_specs=[rowspec] * 5 + [pl.BlockSpec((1, tc, nv, ln), lambda gi, ti: (gi, ti, 0, 0)), sspec],
        out_specs=[pl.BlockSpec((1, tc, nv, 64), lambda gi, ti: (gi, ti, 0, 0)), sspec],
        out_shape=[jax.ShapeDtypeStruct((g, t, nv, 64), F32), jax.ShapeDtypeStruct(s0.shape, F32)],
        scratch_shapes=[pltpu.VMEM((nk2, nv, ln), F32)],
        compiler_params=_params(("arbitrary", "arbitrary")),
        name="rwkv_scan",
    )(r, w, k, nkk, kka, v2, s0)


def _to_scan_rows(a, groups, t):
    a = a.reshape(groups, 8, t, 8, 64).transpose(0, 2, 4, 1, 3)
    return a.reshape(groups, t, 32, 128)


def _to_scan_cols(a, groups, t):
    a = a.reshape(groups, 8, t, 8, 64).transpose(0, 2, 4, 1, 3).reshape(groups, t, 64, 64)
    return jnp.concatenate([a, a], axis=-1)


def _state_to_scan(s, groups):
    s = s.reshape(groups, 8, 8, 64, 32, 2).transpose(0, 4, 3, 5, 1, 2)
    return s.reshape(groups, 32, 64, 128)


def _state_from_scan(s, groups):
    s = s.reshape(groups, 32, 64, 2, 8, 8).transpose(0, 4, 5, 2, 1, 3)
    return s.reshape(groups * 8, 8, 64, 64)


def _y_from_scan(y, groups, t):
    y = y.reshape(groups, t, 64, 8, 8).transpose(0, 3, 1, 4, 2)
    return y.reshape(groups * 8 * t, 8 * 64)


def _cum_kernel(lf_ref, c_ref, *, blk):
    t = lf_ref.shape[2]
    src = lax.broadcasted_iota(jnp.int32, (blk, blk), 0)
    dst = lax.broadcasted_iota(jnp.int32, (blk, blk), 1)
    tri = (src <= dst).astype(BF16)
    carry = jnp.zeros((lf_ref.shape[1], 1), F32)
    for i in range(t // blk):
        cs = _dot_exact_rhs(lf_ref[0, :, i * blk:(i + 1) * blk], tri) + carry
        c_ref[0, :, i * blk:(i + 1) * blk] = cs
        carry = cs[:, blk - 1:blk]


def _cumsum_t(lf_t):
    b, h, t = lf_t.shape
    blk = min(256, t)
    spec = pl.BlockSpec((1, h, t), lambda i: (i, 0, 0))
    return pl.pallas_call(
        functools.partial(_cum_kernel, blk=blk),
        grid=(b,), in_specs=[spec], out_specs=spec,
        out_shape=jax.ShapeDtypeStruct((b, h, t), F32),
        compiler_params=_params(("arbitrary",)),
        name="logf_cumsum",
    )(lf_t)


def _attn_kernel(q_ref, k_ref, v_ref, cc_ref, cr_ref, o_ref, *, tq):
    qi = pl.program_id(2)
    q = (q_ref[0, 0] * (HEAD_DIM ** -0.5)).astype(BF16)
    ci = cc_ref[0, 0]

    def block(j, carry, masked):
        m, l, acc = carry
        start = pl.multiple_of(j * tq, tq)
        kj = k_ref[0, 0, pl.ds(start, tq), :].astype(BF16)
        vj = v_ref[0, 0, pl.ds(start, tq), :].astype(BF16)
        s = _dot_nt(q, kj) + ci - cr_ref[0, 0, j]
        if masked:
            row = lax.broadcasted_iota(jnp.int32, (tq, tq), 0)
            col = lax.broadcasted_iota(jnp.int32, (tq, tq), 1)
            s = jnp.where(col <= row, s, NEG_INF)
        m_new = jnp.maximum(m, jnp.max(s, axis=-1, keepdims=True))
        alpha = jnp.exp(m - m_new)
        p = jnp.exp(s - m_new)
        l = l * alpha + jnp.sum(p, axis=-1, keepdims=True)
        acc = acc * alpha + _dot(p.astype(BF16), vj)
        return m_new, l, acc

    init = (jnp.full((tq, 1), NEG_INF, F32), jnp.zeros((tq, 1), F32), jnp.zeros((tq, HEAD_DIM), F32))
    carry = lax.fori_loop(0, qi, lambda j, c: block(j, c, False), init)
    m, l, acc = block(qi, carry, True)
    o_ref[0, 0] = acc / l


def _fox_prompt(q, k, v, c_t, tq):
    b, h, t, d = q.shape
    nq = t // tq
    c_col = c_t[..., None]
    c_row = c_t.reshape(b, h, nq, 1, tq)
    qspec = pl.BlockSpec((1, 1, tq, d), lambda bi, hi, i: (bi, hi, i, 0))
    kvspec = pl.BlockSpec((1, 1, t, d), lambda bi, hi, i: (bi, hi, 0, 0))
    return pl.pallas_call(
        functools.partial(_attn_kernel, tq=tq),
        grid=(b, h, nq),
        in_specs=[qspec, kvspec, kvspec,
                  pl.BlockSpec((1, 1, tq, 1), lambda bi, hi, i: (bi, hi, i, 0)),
                  pl.BlockSpec((1, 1, nq, 1, tq), lambda bi, hi, i: (bi, hi, 0, 0, 0))],
        out_specs=qspec,
        out_shape=jax.ShapeDtypeStruct((b, h, t, d), F32),
        compiler_params=_params(("arbitrary", "arbitrary", "arbitrary")),
        name="fox_prompt",
    )(q, k, v, c_col, c_row)


def _suffix_kernel(lf_ref, suf_ref, tot_ref):
    page = lf_ref.shape[1]
    src = lax.broadcasted_iota(jnp.int32, (page, page), 0)
    dst = lax.broadcasted_iota(jnp.int32, (page, page), 1)
    hi, mid, lo = _split3(lf_ref[...])
    later = (src > dst).astype(BF16)
    ones = jnp.ones((page, page), BF16)
    suf_ref[...] = _dot(hi, later) + _dot(mid, later) + _dot(lo, later)
    tot_ref[...] = _dot(hi, ones) + _dot(mid, ones) + _dot(lo, ones)


def _page_suffix(lf_rows):
    n, page = lf_rows.shape
    rt = math.gcd(n, 2048)
    spec = pl.BlockSpec((rt, page), lambda i: (i, 0))
    return pl.pallas_call(
        _suffix_kernel,
        grid=(n // rt,), in_specs=[spec], out_specs=[spec, spec],
        out_shape=[jax.ShapeDtypeStruct((n, page), F32)] * 2,
        compiler_params=_params(("arbitrary",)),
        name="logf_page_suffix",
    )(lf_rows)


def _dec_kernel(pt_ref, q_ref, kn_ref, vn_ref, lfn_ref, *refs, npg):
    k_refs = refs[:npg]
    v_refs = refs[npg:2 * npg]
    suf_refs = refs[2 * npg:3 * npg]
    tot_refs = refs[3 * npg:4 * npg]
    o_ref = refs[4 * npg]
    m_s, l_s, acc_s, car_s = refs[4 * npg + 1:]
    s_id = pl.program_id(1)
    q = q_ref[0] * (HEAD_DIM ** -0.5)

    @pl.when(s_id == 0)
    def _():
        m_s[...] = jnp.sum(q * kn_ref[0], axis=1, keepdims=True)[:, :, :1]
        l_s[...] = jnp.ones(l_s.shape, F32)
        acc_s[...] = vn_ref[0]
        car_s[...] = jnp.zeros(car_s.shape, F32)

    carry = car_s[...]
    base = lfn_ref[0]
    logits = []
    for i in range(npg):
        qk = jnp.sum(q * k_refs[i][0], axis=1, keepdims=True)
        logits.append(qk + (suf_refs[i][0] + (carry + base)))
        carry = carry + tot_refs[i][0]
    car_s[...] = carry
    m_old = m_s[...]
    m_new = m_old
    for s in logits:
        m_new = jnp.maximum(m_new, jnp.max(s, axis=2, keepdims=True))
    alpha = jnp.exp(m_old - m_new)
    l_new = l_s[...] * alpha
    acc = acc_s[...] * alpha
    for i in range(npg):
        p = jnp.exp(logits[i] - m_new)
        l_new = l_new + jnp.sum(p, axis=2, keepdims=True)
        acc = acc + v_refs[i][0] * p
    l_s[...] = l_new
    m_s[...] = m_new
    acc_s[...] = acc

    @pl.when(s_id == pl.num_programs(1) - 1)
    def _():
        o_ref[0] = jnp.sum(acc, axis=2, keepdims=True) / l_new


def _fox_decode(q, k_new, v_new, lf_new, cache_k, cache_v, suf, tot, page_table):
    bd, nh, hd = q.shape
    n_pages = page_table.shape[1]
    page = cache_k.shape[3]
    npg = PAGES_PER_STEP
    assert n_pages % npg == 0 and page == LANES
    steps = n_pages // npg

    def page_idx(i):
        return lambda b, s, pt: (pt[b, n_pages - 1 - (s * npg + i)], 0, 0, 0)

    on_lanes = lambda a: jnp.broadcast_to(a[..., None], a.shape + (LANES,))
    lane0 = lambda a: jnp.pad(a[..., None], ((0, 0),) * a.ndim + ((0, LANES - 1),))
    featspec = pl.BlockSpec((1, nh, hd, LANES), lambda b, s, pt: (b, 0, 0, 0))
    headspec = pl.BlockSpec((1, nh, 1, LANES), lambda b, s, pt: (b, 0, 0, 0))
    in_specs = [featspec, featspec, featspec, headspec]
    in_specs += [pl.BlockSpec((1, nh, hd, page), page_idx(i)) for i in range(npg)]
    in_specs += [pl.BlockSpec((1, nh, hd, page), page_idx(i)) for i in range(npg)]
    in_specs += [pl.BlockSpec((1, nh, 1, page), page_idx(i)) for i in range(npg)]
    in_specs += [pl.BlockSpec((1, nh, 1, page), page_idx(i)) for i in range(npg)]
    grid_spec = pltpu.PrefetchScalarGridSpec(
        num_scalar_prefetch=1, grid=(bd, steps), in_specs=in_specs,
        out_specs=pl.BlockSpec((1, nh, hd, 1), lambda b, s, pt: (b, 0, 0, 0)),
        scratch_shapes=[pltpu.VMEM((nh, 1, 1), F32), pltpu.VMEM((nh, 1, 1), F32),
                        pltpu.VMEM((nh, hd, LANES), F32), pltpu.VMEM((nh, 1, LANES), F32)])
    return pl.pallas_call(
        functools.partial(_dec_kernel, npg=npg),
        grid_spec=grid_spec,
        out_shape=jax.ShapeDtypeStruct((bd, nh, hd, 1), F32),
        compiler_params=_params(("arbitrary", "arbitrary")),
        name="fox_decode",
    )(page_table, on_lanes(q), on_lanes(k_new), lane0(v_new), on_lanes(lf_new)[:, :, None, :],
      *([cache_k] * npg), *([cache_v] * npg), *([suf] * npg), *([tot] * npg))


def _out_kernel(y_ref, bonus_ref, g_ref, fox_ref, x_ref, gt_ref, sc_ref, sh_ref,
                gnw_ref, gnb_ref, seg_ref, wo_ref, g2_ref, x1_ref, h2b_ref, *, rw):
    seg = seg_ref[...]
    y = y_ref[...]
    mean = _dot_exact_rhs(y, seg) * (1.0 / HEAD_DIM)
    d = y - mean
    var = _dot_exact_rhs(d * d, seg) * (1.0 / HEAD_DIM)
    yn = d * lax.rsqrt(var + GN_EPS) * gnw_ref[...] + gnb_ref[...]
    ro = (yn + bonus_ref[...]) * g_ref[...]
    mixed = _dot(ro.astype(BF16), wo_ref[:rw, :]) + _dot(fox_ref[...].astype(BF16), wo_ref[rw:, :])
    x1 = x_ref[...] + gt_ref[0] * mixed
    h2 = _rms(x1, g2_ref[...]) * (1.0 + sc_ref[0]) + sh_ref[0]
    x1_ref[...] = x1
    h2b_ref[...] = h2.astype(BF16)


def _out_proj(y, bonus, g, fox, x, gt, sc, sh, gnw, gnb, seg, wo, g2, rt, rows_per_group):
    n, d = x.shape
    rw = y.shape[1]
    row = lambda w: pl.BlockSpec((rt, w), lambda i: (i, 0))
    full = lambda a: pl.BlockSpec(a.shape, lambda i: (0,) * a.ndim)
    ms = lambda m: _mod_spec(m, rt, rows_per_group)
    return pl.pallas_call(
        functools.partial(_out_kernel, rw=rw),
        grid=(n // rt,),
        in_specs=[row(rw), row(rw), row(rw), row(fox.shape[1]), row(d), ms(gt), ms(sc), ms(sh),
                  full(gnw), full(gnb), full(seg), full(wo), full(g2)],
        out_specs=[row(d), row(d)],
        out_shape=[jax.ShapeDtypeStruct((n, d), F32), jax.ShapeDtypeStruct((n, d), BF16)],
        compiler_params=_params(("arbitrary",)),
        name="out_proj",
    )(y, bonus, g, fox, x, gt, sc, sh, gnw, gnb, seg, wo, g2)


def _top16(s, rows):
    iota = lax.broadcasted_iota(jnp.int32, s.shape, 0).astype(F32)
    rank = jnp.full(s.shape, float(TOPK), F32)
    cur = s
    vals = []
    for i in range(TOPK):
        m = jnp.max(cur, axis=0, keepdims=True)
        idx = jnp.min(jnp.where(cur == m, iota, float(rows)), axis=0, keepdims=True)
        sel = iota == idx
        rank = jnp.where(sel, float(i), rank)
        cur = jnp.where(sel, NEG_INF, cur)
        vals.append(m)
    return vals, rank


def _route_kernel(h_ref, wq_ref, sk_ref, r2_ref, e2_ref, c1_ref, e1_ref, cand, selm, *, nheads):
    rt = h_ref.shape[0]
    q_t = _dot_nt(wq_ref[...], h_ref[...])
    cand[_CAND_ROWS - 8:, :] = jnp.full((8, rt), NEG_INF, F32)
    iota_c = lax.broadcasted_iota(jnp.int32, (_CAND_ROWS, rt), 0).astype(F32)
    for h in range(nheads):
        qa = q_t[(2 * h) * N_KEYS:(2 * h + 1) * N_KEYS].astype(BF16)
        qb = q_t[(2 * h + 1) * N_KEYS:(2 * h + 2) * N_KEYS].astype(BF16)
        s1 = _dot(sk_ref[2 * h], qa)
        s2 = _dot(sk_ref[2 * h + 1], qb)
        v1, rank1 = _top16(s1, N_KEYS)
        v2, rank2 = _top16(s2, N_KEYS)
        for row, (i, j) in enumerate(_PAIRS):
            cand[row:row + 1, :] = v1[i] + v2[j]
        cur = cand[...]
        top = cur[0:1, :]
        sel_all = jnp.zeros(cur.shape, F32)
        for _ in range(TOPK):
            m = jnp.max(cur, axis=0, keepdims=True)
            idx = jnp.min(jnp.where(cur == m, iota_c, float(_CAND_ROWS)), axis=0, keepdims=True)
            sel = iota_c == idx
            sel_all = jnp.where(sel, 1.0, sel_all)
            cur = jnp.where(sel, NEG_INF, cur)
        z = jnp.sum(sel_all * jnp.exp(cand[...] - top), axis=0, keepdims=True)
        selm[...] = sel_all
        c1 = jnp.zeros(rank1.shape, F32)
        row = 0
        for i in range(TOPK):
            n_i = TOPK // (i + 1)
            cnt = jnp.sum(selm[row:row + n_i, :], axis=0, keepdims=True)
            c1 = jnp.where(rank1 == float(i), cnt, c1)
            row += n_i
        e1 = jnp.where(rank1 < float(TOPK), jnp.exp(s1 - v1[0]), 0.0) * (1.0 / z)
        e2 = jnp.where(rank2 < float(TOPK), jnp.exp(s2 - v2[0]), 0.0)
        r2_ref[h] = rank2.astype(BF16)
        e2_ref[h] = e2.astype(BF16)
        c1_ref[h] = c1
        e1_ref[h] = e1


def _peer_route(h2b, wq_t, sk, rt):
    n, d = h2b.shape
    nheads = sk.shape[0] // 2
    tab = pl.BlockSpec((nheads, N_KEYS, rt), lambda i: (0, 0, i))
    full = lambda a: pl.BlockSpec(a.shape, lambda i: (0,) * a.ndim)
    return pl.pallas_call(
        functools.partial(_route_kernel, nheads=nheads),
        grid=(n // rt,),
        in_specs=[pl.BlockSpec((rt, d), lambda i: (i, 0)), full(wq_t), full(sk)],
        out_specs=[tab] * 4,
        out_shape=[jax.ShapeDtypeStruct((nheads, N_KEYS, n), BF16)] * 2
        + [jax.ShapeDtypeStruct((nheads, N_KEYS, n), F32)] * 2,
        scratch_shapes=[pltpu.VMEM((_CAND_ROWS, rt), F32), pltpu.VMEM((_CAND_ROWS, rt), F32)],
        compiler_params=_params(("arbitrary",)),
        name="peer_route",
    )(h2b, wq_t, sk)


def _gelu_tanh(x):
    return 0.5 * x * (1.0 + jnp.tanh(math.sqrt(2.0 / math.pi) * (x + 0.044715 * (x * x * x))))


def _peer_kernel(h_ref, u_ref, vt_ref, r2_ref, e2_ref, c1_ref, e1_ref, o_ref, *, a_per, nheads):
    j = pl.program_id(1)

    @pl.when(j == 0)
    def _():
        o_ref[...] = jnp.zeros(o_ref.shape, F32)

    hb = h_ref[...]
    sub = 2 * N_KEYS
    acc = None
    for sb in range(a_per // 2):
        h_t = _dot_nt(u_ref[sb * sub:(sb + 1) * sub, :], hb)
        parts = []
        for a2 in range(2):
            a_glob = j * a_per + sb * 2 + a2
            act = _gelu_tanh(h_t[a2 * N_KEYS:(a2 + 1) * N_KEYS]).astype(BF16)
            gate = None
            for h in range(nheads):
                cnt = c1_ref[h, pl.ds(a_glob, 1), :].astype(BF16)
                w1 = e1_ref[h, pl.ds(a_glob, 1), :].astype(BF16)
                term = jnp.where(r2_ref[h] < cnt, e2_ref[h], jnp.zeros((), BF16)) * w1
                gate = term if gate is None else gate + term
            parts.append(gate * act)
        contrib = _dot(vt_ref[:, sb * sub:(sb + 1) * sub], jnp.concatenate(parts, axis=0))
        acc = contrib if acc is None else acc + contrib
    o_ref[...] += acc


def _peer_dense(h2b, u_b, v_t, r2, e2, c1, e1, rt, a_per):
    n, d = h2b.shape
    n_exp = u_b.shape[0]
    nheads = r2.shape[0]
    et = a_per * N_KEYS
    tab = pl.BlockSpec((nheads, N_KEYS, rt), lambda i, j: (0, 0, i))
    return pl.pallas_call(
        functools.partial(_peer_kernel, a_per=a_per, nheads=nheads),
        grid=(n // rt, n_exp // et),
        in_specs=[pl.BlockSpec((rt, d), lambda i, j: (i, 0)),
                  pl.BlockSpec((et, d), lambda i, j: (j, 0)),
                  pl.BlockSpec((d, et), lambda i, j: (0, j)),
                  tab, tab, tab, tab],
        out_specs=pl.BlockSpec((d, rt), lambda i, j: (0, i)),
        out_shape=jax.ShapeDtypeStruct((d, n), F32),
        compiler_params=_params(("arbitrary", "arbitrary")),
        name="peer_dense",
    )(h2b, u_b, v_t, r2, e2, c1, e1)


def _fin_kernel(x_ref, f_ref, gt_ref, g_ref, o_ref):
    o_ref[...] = _rms(x_ref[...] + gt_ref[0] * f_ref[...], g_ref[...])


def _final(x1, ffn, gt, gf, rt, rows_per_group):
    n, d = x1.shape
    row = pl.BlockSpec((rt, d), lambda i: (i, 0))
    return pl.pallas_call(
        _fin_kernel,
        grid=(n // rt,),
        in_specs=[row, row, _mod_spec(gt, rt, rows_per_group), pl.BlockSpec(gf.shape, lambda i: (0, 0))],
        out_specs=row,
        out_shape=jax.ShapeDtypeStruct((n, d), F32),
        compiler_params=_params(("arbitrary",)),
        name="final_norm",
    )(x1, ffn, gt, gf)


def _layer(x, mods, per_row, fox_fn, shift_prev, s0, lw, t, rt, peer_rt):
    n, d = x.shape
    nb = n // t
    rows_per_group = n if per_row else t
    sh1, sc1, gt1, sh2, sc2, gt2 = mods
    p, q, k, v, lf = _in_proj(x, lw["g1"], sc1, sh1, lw["wr"], lw["wqkv"], lw["wf"], lw["bf"], rt,
                              rows_per_group)
    rc = p.shape[1]
    p3 = p.reshape(nb, t, rc)
    pprev = jnp.concatenate([shift_prev[:, None, :], p3[:, :-1]], axis=1).reshape(n, rc)
    r, w, k2, vv, nkk, kka, g, bonus = _rwkv_prep(p, pprev, lw["vecs"], lw["w2p"], lw["a2p"], lw["g2"],
                                                  lw["seg"], rt)
    groups = nb // 8
    tc = min(64, t)
    rows = [_to_scan_rows(a, groups, t) for a in (r, w, k2, nkk, kka)]
    y, s_new = _rwkv_scan(*rows, _to_scan_cols(vv, groups, t), _state_to_scan(s0, groups), tc)
    y = _y_from_scan(y, groups, t)
    s_new = _state_from_scan(s_new, groups)
    fox = fox_fn(q, k, v, lf)
    x1, h2b = _out_proj(y, bonus, g, fox, x, gt1, sc2, sh2, lw["gnw"], lw["gnb"], lw["seg"], lw["wo"],
                        lw["g2n"], rt, rows_per_group)
    r2, e2, c1, e1 = _peer_route(h2b, lw["wq_t"], lw["sk"], peer_rt)
    ffn_t = _peer_dense(h2b, lw["u_b"], lw["v_t"], r2, e2, c1, e1, peer_rt, PEER_A_PER_STEP)
    return x1, ffn_t.T, gt2, k, v, lf, s_new, p3[:, -1]


def kernel(x_prompt, x_sample, cache_k, cache_v, cache_logf, state_rwkv, state_shift, page_table, c_prompt, c_sample, w_ada, b_ada, norm1_g, norm2_g, w_in, rwkv_mu, rwkv_w0, rwkv_w2, rwkv_a0, rwkv_a2, rwkv_g2, rwkv_kk, rwkv_ka, rwkv_rk, rwkv_gn_w, rwkv_gn_b, fox_bf, w_out, peer_wq, peer_subkeys, peer_u, peer_v, normf_g):
    depth = w_in.shape[0]
    b, t, d = x_prompt.shape
    bd, td, _ = x_sample.shape
    assert td == 1 and depth == 1
    rw = rwkv_w0.shape[1]
    lora_w, lora_a = rwkv_w2.shape[1], rwkv_a2.shape[1]
    assert lora_w + lora_a == LANES
    rc = rwkv_mu.shape[1]
    nh = fox_bf.shape[1]
    fw = nh * HEAD_DIM
    n_pool, page = cache_k.shape[1], cache_k.shape[2]

    xp = x_prompt.reshape(b * t, d)
    xs = x_sample.reshape(bd, d)
    seg = (jnp.arange(rw)[:, None] // HEAD_DIM == jnp.arange(rw)[None, :] // HEAD_DIM).astype(BF16)
    new_p, new_s = [], []
    for layer in range(depth):
        wl = w_in[layer]
        lw = {
            "g1": norm1_g[layer].reshape(1, d), "g2n": norm2_g[layer].reshape(1, d),
            "wr": wl[:, :rc].astype(BF16), "wqkv": wl[:, rc:rc + 3 * fw].astype(BF16),
            "wf": jnp.pad(wl[:, rc + 3 * fw:], ((0, 0), (0, LANES - nh))).astype(BF16),
            "bf": fox_bf[layer].reshape(1, nh),
            "vecs": [rwkv_mu[layer].reshape(1, rc), rwkv_w0[layer].reshape(1, rw), rwkv_a0[layer].reshape(1, rw),
                     rwkv_kk[layer].reshape(1, rw), rwkv_ka[layer].reshape(1, rw), rwkv_rk[layer].reshape(1, rw)],
            "w2p": jnp.pad(rwkv_w2[layer], ((0, lora_a), (0, 0))).astype(BF16),
            "a2p": jnp.pad(rwkv_a2[layer], ((lora_w, 0), (0, 0))).astype(BF16),
            "g2": rwkv_g2[layer].astype(BF16), "seg": seg,
            "gnw": rwkv_gn_w[layer].reshape(1, rw), "gnb": rwkv_gn_b[layer].reshape(1, rw),
            "wo": w_out[layer].astype(BF16),
            "wq_t": peer_wq[layer].T.astype(BF16),
            "sk": peer_subkeys[layer].reshape(-1, N_KEYS, peer_subkeys.shape[-1]).astype(BF16),
            "u_b": peer_u[layer].astype(BF16), "v_t": peer_v[layer].T.astype(BF16),
        }
        mod = _adaln(jnp.concatenate([c_prompt, c_sample], axis=0), w_ada[layer], b_ada[layer])
        mods_p = [mod[:b, i * d:(i + 1) * d].reshape(b, 1, d) for i in range(6)]
        mods_s = [mod[b:, i * d:(i + 1) * d].reshape(1, bd, d) for i in range(6)]

        def fox_p(q, k, v, lf):
            heads = lambda a: a.reshape(b, t, nh, HEAD_DIM).transpose(0, 2, 1, 3)
            c_t = _cumsum_t(lf.reshape(b, t, nh).transpose(0, 2, 1))
            o = _fox_prompt(heads(q), heads(k), heads(v), c_t, min(256, t))
            return o.transpose(0, 2, 1, 3).reshape(b * t, fw)

        def fox_s(q, k, v, lf, layer=layer):
            heads = lambda a: a.reshape(bd, nh, HEAD_DIM)
            by_row = lambda a: a.reshape(n_pool, nh, 1, page)
            suf, tot = _page_suffix(cache_logf[layer].transpose(0, 2, 1).reshape(n_pool * nh, page))
            o = _fox_decode(heads(q), heads(k), heads(v), lf,
                            cache_k[layer].transpose(0, 2, 3, 1), cache_v[layer].transpose(0, 2, 3, 1),
                            by_row(suf), by_row(tot), page_table)
            return o.reshape(bd, fw)

        rt_p = min(512, t)
        x1p, ffn_p, gt2p, k_p, v_p, lf_p, st_p, sh_p = _layer(
            xp, mods_p, False, fox_p, jnp.zeros((b, rc), F32), jnp.zeros((b, rw // HEAD_DIM, HEAD_DIM, HEAD_DIM), F32),
            lw, t, rt_p, min(256, t))
        x1s, ffn_s, gt2s, k_s, v_s, lf_s, st_s, sh_s = _layer(
            xs, mods_s, True, fox_s, state_shift[layer], state_rwkv[layer], lw, 1, bd, bd)
        gf = normf_g.reshape(1, d)
        xp = _final(x1p, ffn_p, gt2p, gf, rt_p, t)
        xs = _final(x1s, ffn_s, gt2s, gf, bd, bd)
        new_p.append((k_p.reshape(b, t, nh, HEAD_DIM), v_p.reshape(b, t, nh, HEAD_DIM), lf_p.reshape(b, t, nh),
                      st_p, sh_p))
        new_s.append((k_s.reshape(bd, 1, nh, HEAD_DIM), v_s.reshape(bd, 1, nh, HEAD_DIM), lf_s.reshape(bd, 1, nh),
                      st_s, sh_s))

    stack = lambda lst, i: jnp.stack([e[i] for e in lst], axis=0)
    return (xp.reshape(b, t, d), xs.reshape(bd, 1, d),
            stack(new_p, 0), stack(new_p, 1), stack(new_p, 2), stack(new_p, 3), stack(new_p, 4),
            stack(new_s, 0), stack(new_s, 1), stack(new_s, 2), stack(new_s, 3), stack(new_s, 4))
```

```python
import functools
import math

import jax
import jax.numpy as jnp
from jax import lax
from jax.experimental import pallas as pl
from jax.experimental.pallas import tpu as pltpu

F32 = jnp.float32
BF16 = jnp.bfloat16

HEAD_DIM = 64
LANES = 128
PAGES_PER_STEP = 8
PEER_A_PER_STEP = 16
ATTN_HEADS_PER_STEP = 4
TOPK = 16
N_KEYS = 128
RMS_EPS = 1e-6
GN_EPS = 64e-5
NEG_INF = float("-inf")
VMEM_LIMIT = 56 * 1024 * 1024

_PAIRS = [(i, j) for i in range(TOPK) for j in range(TOPK) if (i + 1) * (j + 1) <= TOPK]
_N_CAND = len(_PAIRS)
_CAND_ROWS = -(-_N_CAND // 8) * 8


def _params(sem, vmem=VMEM_LIMIT):
    return pltpu.CompilerParams(dimension_semantics=sem, vmem_limit_bytes=vmem)


def _dot(a, b):
    return jnp.dot(a, b, preferred_element_type=F32)


def _dot_nt(a, b):
    return lax.dot_general(a, b, (((1,), (1,)), ((), ())), preferred_element_type=F32)


def _split3(x):
    hi = x.astype(BF16)
    r1 = x - hi.astype(F32)
    mid = r1.astype(BF16)
    lo = (r1 - mid.astype(F32)).astype(BF16)
    return hi, mid, lo


def _dot_exact_rhs(x, m):
    hi, mid, lo = _split3(x)
    return _dot(hi, m) + _dot(mid, m) + _dot(lo, m)


def _softplus(z):
    return jnp.maximum(z, 0.0) + jnp.log1p(jnp.exp(-jnp.abs(z)))


def _sigmoid(z):
    return 1.0 / (1.0 + jnp.exp(-z))


def _rms(x, g):
    ms = jnp.mean(x * x, axis=-1, keepdims=True)
    return x * lax.rsqrt(ms + RMS_EPS) * g


def _ada_kernel(c_ref, w_ref, b_ref, o_ref):
    c = c_ref[...]
    s = c * _sigmoid(c)
    o_ref[...] = _dot(s.astype(BF16), w_ref[...].astype(BF16)) + b_ref[...]


def _adaln(c, w_ada, b_ada):
    n, d = c.shape
    cols = w_ada.shape[1]
    tn = 1024
    return pl.pallas_call(
        _ada_kernel,
        grid=(cols // tn,),
        in_specs=[pl.BlockSpec((n, d), lambda j: (0, 0)),
                  pl.BlockSpec((d, tn), lambda j: (0, j)),
                  pl.BlockSpec((1, tn), lambda j: (0, j))],
        out_specs=pl.BlockSpec((n, tn), lambda j: (0, j)),
        out_shape=jax.ShapeDtypeStruct((n, cols), F32),
        compiler_params=_params(("arbitrary",)),
        name="adaln",
    )(c, w_ada, b_ada.reshape(1, cols))


def _in_kernel(x_ref, g_ref, sc_ref, sh_ref, wr_ref, wqkv_ref, wf_ref, bf_ref,
               p_ref, q_ref, k_ref, v_ref, lf_ref, *, fw):
    h = _rms(x_ref[...], g_ref[...]) * (1.0 + sc_ref[0]) + sh_ref[0]
    hb = h.astype(BF16)
    p_ref[...] = _dot(hb, wr_ref[...])
    qkv = _dot(hb, wqkv_ref[...])
    q_ref[...] = qkv[:, :fw]
    k_ref[...] = qkv[:, fw:2 * fw]
    v_ref[...] = qkv[:, 2 * fw:]
    f = _dot(hb, wf_ref[...])[:, :bf_ref.shape[1]] + bf_ref[...]
    lf_ref[...] = -_softplus(-f)


def _mod_spec(mod, rt, rows_per_group):
    mr, d = mod.shape[1], mod.shape[2]
    tpg = rows_per_group // rt
    return pl.BlockSpec((1, mr, d), lambda i: (i // tpg, 0, 0))


def _in_proj(x, g1, sc, sh, wr, wqkv, wf, bfox, rt, rows_per_group):
    n, d = x.shape
    rc, fw3, nh = wr.shape[1], wqkv.shape[1], bfox.shape[1]
    fw = fw3 // 3
    row = lambda w: pl.BlockSpec((rt, w), lambda i: (i, 0))
    full = lambda a: pl.BlockSpec(a.shape, lambda i: (0,) * a.ndim)
    return pl.pallas_call(
        functools.partial(_in_kernel, fw=fw),
        grid=(n // rt,),
        in_specs=[row(d), full(g1), _mod_spec(sc, rt, rows_per_group), _mod_spec(sh, rt, rows_per_group),
                  full(wr), full(wqkv), full(wf), full(bfox)],
        out_specs=[row(rc), row(fw), row(fw), row(fw), row(nh)],
        out_shape=[jax.ShapeDtypeStruct((n, rc), F32), jax.ShapeDtypeStruct((n, fw), F32),
                   jax.ShapeDtypeStruct((n, fw), F32), jax.ShapeDtypeStruct((n, fw), F32),
                   jax.ShapeDtypeStruct((n, nh), F32)],
        compiler_params=_params(("arbitrary",)),
        name="in_proj",
    )(x, g1, sc, sh, wr, wqkv, wf, bfox)


def _prep_kernel(p_ref, pp_ref, mu_ref, w0_ref, a0_ref, kkp_ref, ka_ref, rk_ref,
                 w2_ref, a2_ref, g2_ref, seg_ref,
                 r_ref, w_ref, k_ref, v_ref, nkk_ref, kka_ref, g_ref, bonus_ref, *, rw):
    p = p_ref[...]
    xs = p + mu_ref[...] * (pp_ref[...] - p)
    r = xs[:, :rw]
    k = xs[:, rw:2 * rw]
    v = xs[:, 2 * rw:3 * rw]
    xwa = xs[:, 3 * rw:3 * rw + LANES]
    xg = xs[:, 3 * rw + LANES:]
    seg = seg_ref[...]
    lw = w0_ref[...] + _dot(jnp.tanh(xwa).astype(BF16), w2_ref[...])
    w_log = -_softplus(-lw) - 0.5
    decay = jnp.exp(-jnp.exp(w_log))
    a = _sigmoid(a0_ref[...] + _dot(xwa.astype(BF16), a2_ref[...]))
    g = _dot(_sigmoid(xg).astype(BF16), g2_ref[...])
    kk = k * kkp_ref[...]
    nrm = jnp.sqrt(_dot_exact_rhs(kk * kk, seg))
    kkn = kk / jnp.maximum(nrm, 1e-12)
    k2 = k * (1.0 + (a - 1.0) * ka_ref[...])
    bonus = _dot_exact_rhs(r * k2 * rk_ref[...], seg) * v
    r_ref[...] = r
    w_ref[...] = decay
    k_ref[...] = k2
    v_ref[...] = v
    nkk_ref[...] = -kkn
    kka_ref[...] = kkn * a
    g_ref[...] = g
    bonus_ref[...] = bonus


def _rwkv_prep(p, pprev, vecs, w2p, a2p, g2, seg, rt):
    n, rc = p.shape
    rw = seg.shape[0]
    row = lambda w: pl.BlockSpec((rt, w), lambda i: (i, 0))
    full = lambda a: pl.BlockSpec(a.shape, lambda i: (0,) * a.ndim)
    return pl.pallas_call(
        functools.partial(_prep_kernel, rw=rw),
        grid=(n // rt,),
        in_specs=[row(rc), row(rc)] + [full(a) for a in vecs] + [full(w2p), full(a2p), full(g2), full(seg)],
        out_specs=[row(rw)] * 8,
        out_shape=[jax.ShapeDtypeStruct((n, rw), F32)] * 8,
        compiler_params=_params(("arbitrary",)),
        name="rwkv_prep",
    )(p, pprev, *vecs, w2p, a2p, g2, seg)


def _scan_kernel(r_ref, w_ref, k_ref, nkk_ref, kka_ref, v_ref, s0_ref, y_ref, sT_ref, S, *, tc, nk2):
    tb = pl.program_id(1)

    @pl.when(tb == 0)
    def _():
        S[...] = s0_ref[0]

    def step(t, carry):
        vv = v_ref[0, t]
        sa = jnp.zeros(vv.shape, F32)
        for k2 in range(nk2):
            sa = sa + S[k2] * nkk_ref[0, t, k2:k2 + 1, :]
        sa = sa + pltpu.roll(sa, 64, axis=1)
        y = jnp.zeros(vv.shape, F32)
        for k2 in range(nk2):
            s = (S[k2] * w_ref[0, t, k2:k2 + 1, :] + sa * kka_ref[0, t, k2:k2 + 1, :]
                 + vv * k_ref[0, t, k2:k2 + 1, :])
            S[k2] = s
            y = y + s * r_ref[0, t, k2:k2 + 1, :]
        y = y + pltpu.roll(y, 64, axis=1)
        y_ref[0, t] = y[:, :64]
        return carry

    lax.fori_loop(0, tc, step, 0)

    @pl.when(tb == pl.num_programs(1) - 1)
    def _():
        sT_ref[0] = S[...]


def _rwkv_scan(r, w, k, nkk, kka, v2, s0, tc):
    g, t, nk2, ln = r.shape
    nv = v2.shape[2]
    rowspec = pl.BlockSpec((1, tc, nk2, ln), lambda gi, ti: (gi, ti, 0, 0))
    sspec = pl.BlockSpec((1, nk2, nv, ln), lambda gi, ti: (gi, 0, 0, 0))
    return pl.pallas_call(
        functools.partial(_scan_kernel, tc=tc, nk2=nk2),
        grid=(g, t // tc),
        in_specs=[rowspec] * 5 + [pl.BlockSpec((1, tc, nv, ln), lambda gi, ti: (gi, ti, 0, 0)), sspec],
        out_specs=[pl.BlockSpec((1, tc, nv, 64), lambda gi, ti: (gi, ti, 0, 0)), sspec],
        out_shape=[jax.ShapeDtypeStruct((g, t, nv, 64), F32), jax.ShapeDtypeStruct(s0.shape, F32)],
        scratch_shapes=[pltpu.VMEM((nk2, nv, ln), F32)],
        compiler_params=_params(("arbitrary", "arbitrary")),
        name="rwkv_scan",
    )(r, w, k, nkk, kka, v2, s0)


def _to_scan_rows(a, groups, t):
    a = a.reshape(groups, 8, t, 8, 64).transpose(0, 2, 4, 1, 3)
    return a.reshape(groups, t, 32, 128)


def _to_scan_cols(a, groups, t):
    a = a.reshape(groups, 8, t, 8, 64).transpose(0, 2, 4, 1, 3).reshape(groups, t, 64, 64)
    return jnp.concatenate([a, a], axis=-1)


def _state_to_scan(s, groups):
    s = s.reshape(groups, 8, 8, 64, 32, 2).transpose(0, 4, 3, 5, 1, 2)
    return s.reshape(groups, 32, 64, 128)


def _state_from_scan(s, groups):
    s = s.reshape(groups, 32, 64, 2, 8, 8).transpose(0, 4, 5, 2, 1, 3)
    return s.reshape(groups * 8, 8, 64, 64)


def _y_from_scan(y, groups, t):
    y = y.reshape(groups, t, 64, 8, 8).transpose(0, 3, 1, 4, 2)
    return y.reshape(groups * 8 * t, 8 * 64)


def _cum_kernel(lf_ref, c_ref, *, blk):
    t = lf_ref.shape[2]
    src = lax.broadcasted_iota(jnp.int32, (blk, blk), 0)
    dst = lax.broadcasted_iota(jnp.int32, (blk, blk), 1)
    tri = (src <= dst).astype(BF16)
    carry = jnp.zeros((lf_ref.shape[1], 1), F32)
    for i in range(t // blk):
        cs = _dot_exact_rhs(lf_ref[0, :, i * blk:(i + 1) * blk], tri) + carry
        c_ref[0, :, i * blk:(i + 1) * blk] = cs
        carry = cs[:, blk - 1:blk]


def _cumsum_t(lf_t):
    b, h, t = lf_t.shape
    blk = min(256, t)
    spec = pl.BlockSpec((1, h, t), lambda i: (i, 0, 0))
    return pl.pallas_call(
        functools.partial(_cum_kernel, blk=blk),
        grid=(b,), in_specs=[spec], out_specs=spec,
        out_shape=jax.ShapeDtypeStruct((b, h, t), F32),
        compiler_params=_params(("arbitrary",)),
        name="logf_cumsum",
    )(lf_t)


def _attn_kernel(q_ref, k_ref, vt_ref, cc_ref, cr_ref, o_ref, *, tq, hb):
    qi = pl.program_id(2)
    qs = [(q_ref[0, g] * (HEAD_DIM ** -0.5)).astype(BF16) for g in range(hb)]
    cis = [cr_ref[0, g, qi] for g in range(hb)]

    def block(j, carry, masked):
        start = pl.multiple_of(j * tq, tq)
        scores = [_dot_nt(k_ref[0, g, pl.ds(start, tq), :].astype(BF16), qs[g]) for g in range(hb)]
        stats = []
        for g in range(hb):
            m, l, _ = carry[g]
            s = scores[g] + (cis[g] - cc_ref[0, g, j])
            if masked:
                key = lax.broadcasted_iota(jnp.int32, (tq, tq), 0)
                qry = lax.broadcasted_iota(jnp.int32, (tq, tq), 1)
                s = jnp.where(key <= qry, s, NEG_INF)
            m_new = jnp.maximum(m, jnp.max(s, axis=0, keepdims=True))
            alpha = jnp.exp(m - m_new)
            p = jnp.exp(s - m_new)
            stats.append((m_new, l * alpha + jnp.sum(p, axis=0, keepdims=True), alpha, p.astype(BF16)))
        out = []
        for g in range(hb):
            m_new, l, alpha, p = stats[g]
            acc = carry[g][2] * alpha + _dot(vt_ref[0, g, j].astype(BF16), p)
            out.append((m_new, l, acc))
        return tuple(out)

    init = tuple((jnp.full((1, tq), NEG_INF, F32), jnp.zeros((1, tq), F32), jnp.zeros((HEAD_DIM, tq), F32))
                 for _ in range(hb))
    carry = lax.fori_loop(0, qi, lambda j, c: block(j, c, False), init)
    carry = block(qi, carry, True)
    for g in range(hb):
        o_ref[0, g] = carry[g][2] / carry[g][1]


def _fox_prompt(q, k, v_t, c_t, tq, hb):
    b, h, t, d = q.shape
    nq = t // tq
    c_col = c_t.reshape(b, h, nq, tq, 1)
    c_row = c_t.reshape(b, h, nq, 1, tq)
    whole = lambda shape: pl.BlockSpec((1, hb) + shape, lambda bi, hi, i: (bi, hi) + (0,) * len(shape))
    return pl.pallas_call(
        functools.partial(_attn_kernel, tq=tq, hb=hb),
        grid=(b, h // hb, nq),
        in_specs=[pl.BlockSpec((1, hb, tq, d), lambda bi, hi, i: (bi, hi, i, 0)),
                  whole((t, d)), whole((nq, d, tq)), whole((nq, tq, 1)), whole((nq, 1, tq))],
        out_specs=pl.BlockSpec((1, hb, d, tq), lambda bi, hi, i: (bi, hi, 0, i)),
        out_shape=jax.ShapeDtypeStruct((b, h, d, t), F32),
        compiler_params=_params(("arbitrary", "arbitrary", "arbitrary")),
        name="fox_prompt",
    )(q, k, v_t, c_col, c_row)


def _suffix_kernel(lf_ref, o_ref):
    page = lf_ref.shape[1]
    src = lax.broadcasted_iota(jnp.int32, (page, page), 0)
    dst = lax.broadcasted_iota(jnp.int32, (page, page), 1)
    hi, mid, lo = _split3(lf_ref[...])
    later = (src > dst).astype(BF16)
    ones = jnp.ones((page, page), BF16)
    o_ref[:, :page] = _dot(hi, later) + _dot(mid, later) + _dot(lo, later)
    o_ref[:, page:] = _dot(hi, ones) + _dot(mid, ones) + _dot(lo, ones)


def _page_suffix(lf_rows):
    n, page = lf_rows.shape
    rt = math.gcd(n, 2048)
    return pl.pallas_call(
        _suffix_kernel,
        grid=(n // rt,),
        in_specs=[pl.BlockSpec((rt, page), lambda i: (i, 0))],
        out_specs=pl.BlockSpec((rt, 2 * page), lambda i: (i, 0)),
        out_shape=jax.ShapeDtypeStruct((n, 2 * page), F32),
        compiler_params=_params(("arbitrary",)),
        name="logf_page_suffix",
    )(lf_rows)


def _dec_kernel(pt_ref, q_ref, kn_ref, vn_ref, lfn_ref, *refs, npg):
    k_refs = refs[:npg]
    v_refs = refs[npg:2 * npg]
    st_refs = refs[2 * npg:3 * npg]
    o_ref = refs[3 * npg]
    m_s, l_s, acc_s, car_s = refs[3 * npg + 1:]
    s_id = pl.program_id(1)
    page = k_refs[0].shape[3]
    q = q_ref[0] * (HEAD_DIM ** -0.5)

    @pl.when(s_id == 0)
    def _():
        m_s[...] = jnp.sum(q * kn_ref[0], axis=1, keepdims=True)[:, :, :1]
        l_s[...] = jnp.ones(l_s.shape, F32)
        acc_s[...] = vn_ref[0]
        car_s[...] = jnp.zeros(car_s.shape, F32)

    carry = car_s[...]
    base = lfn_ref[0]
    logits = []
    for i in range(npg):
        qk = jnp.sum(q * k_refs[i][0], axis=1, keepdims=True)
        logits.append(qk + (st_refs[i][0, :, :, :page] + (carry + base)))
        carry = carry + st_refs[i][0, :, :, page:]
    car_s[...] = carry
    m_old = m_s[...]
    m_new = m_old
    for s in logits:
        m_new = jnp.maximum(m_new, jnp.max(s, axis=2, keepdims=True))
    alpha = jnp.exp(m_old - m_new)
    l_new = l_s[...] * alpha
    acc = acc_s[...] * alpha
    for i in range(npg):
        p = jnp.exp(logits[i] - m_new)
        l_new = l_new + jnp.sum(p, axis=2, keepdims=True)
        acc = acc + v_refs[i][0] * p
    l_s[...] = l_new
    m_s[...] = m_new
    acc_s[...] = acc

    @pl.when(s_id == pl.num_programs(1) - 1)
    def _():
        o_ref[0] = jnp.sum(acc, axis=2, keepdims=True) / l_new


def _fox_decode(q, k_new, v_new, lf_new, cache_k, cache_v, suf_tot, page_table):
    bd, nh, hd = q.shape
    n_pages = page_table.shape[1]
    page = cache_k.shape[3]
    npg = PAGES_PER_STEP
    assert n_pages % npg == 0 and page == LANES
    steps = n_pages // npg

    def page_idx(i):
        return lambda b, s, pt: (pt[b, n_pages - 1 - (s * npg + i)], 0, 0, 0)

    on_lanes = lambda a: jnp.broadcast_to(a[..., None], a.shape + (LANES,))
    lane0 = lambda a: jnp.pad(a[..., None], ((0, 0),) * a.ndim + ((0, LANES - 1),))
    featspec = pl.BlockSpec((1, nh, hd, LANES), lambda b, s, pt: (b, 0, 0, 0))
    headspec = pl.BlockSpec((1, nh, 1, LANES), lambda b, s, pt: (b, 0, 0, 0))
    in_specs = [featspec, featspec, featspec, headspec]
    in_specs += [pl.BlockSpec((1, nh, hd, page), page_idx(i)) for i in range(npg)]
    in_specs += [pl.BlockSpec((1, nh, hd, page), page_idx(i)) for i in range(npg)]
    in_specs += [pl.BlockSpec((1, nh, 1, 2 * page), page_idx(i)) for i in range(npg)]
    grid_spec = pltpu.PrefetchScalarGridSpec(
        num_scalar_prefetch=1, grid=(bd, steps), in_specs=in_specs,
        out_specs=pl.BlockSpec((1, nh, hd, 1), lambda b, s, pt: (b, 0, 0, 0)),
        scratch_shapes=[pltpu.VMEM((nh, 1, 1), F32), pltpu.VMEM((nh, 1, 1), F32),
                        pltpu.VMEM((nh, hd, LANES), F32), pltpu.VMEM((nh, 1, LANES), F32)])
    return pl.pallas_call(
        functools.partial(_dec_kernel, npg=npg),
        grid_spec=grid_spec,
        out_shape=jax.ShapeDtypeStruct((bd, nh, hd, 1), F32),
        compiler_params=_params(("arbitrary", "arbitrary")),
        name="fox_decode",
    )(page_table, on_lanes(q), on_lanes(k_new), lane0(v_new), on_lanes(lf_new)[:, :, None, :],
      *([cache_k] * npg), *([cache_v] * npg), *([suf_tot] * npg))


def _out_kernel(y_ref, bonus_ref, g_ref, fox_ref, x_ref, gt_ref, sc_ref, sh_ref,
                gnw_ref, gnb_ref, seg_ref, wo_ref, g2_ref, x1_ref, h2b_ref, *, rw):
    seg = seg_ref[...]
    y = y_ref[...]
    mean = _dot_exact_rhs(y, seg) * (1.0 / HEAD_DIM)
    d = y - mean
    var = _dot_exact_rhs(d * d, seg) * (1.0 / HEAD_DIM)
    yn = d * lax.rsqrt(var + GN_EPS) * gnw_ref[...] + gnb_ref[...]
    ro = (yn + bonus_ref[...]) * g_ref[...]
    mixed = _dot(ro.astype(BF16), wo_ref[:rw, :]) + _dot(fox_ref[...].astype(BF16), wo_ref[rw:, :])
    x1 = x_ref[...] + gt_ref[0] * mixed
    h2 = _rms(x1, g2_ref[...]) * (1.0 + sc_ref[0]) + sh_ref[0]
    x1_ref[...] = x1
    h2b_ref[...] = h2.astype(BF16)


def _out_proj(y, bonus, g, fox, x, gt, sc, sh, gnw, gnb, seg, wo, g2, rt, rows_per_group):
    n, d = x.shape
    rw = y.shape[1]
    row = lambda w: pl.BlockSpec((rt, w), lambda i: (i, 0))
    full = lambda a: pl.BlockSpec(a.shape, lambda i: (0,) * a.ndim)
    ms = lambda m: _mod_spec(m, rt, rows_per_group)
    return pl.pallas_call(
        functools.partial(_out_kernel, rw=rw),
        grid=(n // rt,),
        in_specs=[row(rw), row(rw), row(rw), row(fox.shape[1]), row(d), ms(gt), ms(sc), ms(sh),
                  full(gnw), full(gnb), full(seg), full(wo), full(g2)],
        out_specs=[row(d), row(d)],
        out_shape=[jax.ShapeDtypeStruct((n, d), F32), jax.ShapeDtypeStruct((n, d), BF16)],
        compiler_params=_params(("arbitrary",)),
        name="out_proj",
    )(y, bonus, g, fox, x, gt, sc, sh, gnw, gnb, seg, wo, g2)


def _top16(s, rows):
    iota = lax.broadcasted_iota(jnp.int32, s.shape, 0).astype(F32)
    rank = jnp.full(s.shape, float(TOPK), F32)
    cur = s
    vals = []
    for i in range(TOPK):
        m = jnp.max(cur, axis=0, keepdims=True)
        idx = jnp.min(jnp.where(cur == m, iota, float(rows)), axis=0, keepdims=True)
        sel = iota == idx
        rank = jnp.where(sel, float(i), rank)
        cur = jnp.where(sel, NEG_INF, cur)
        vals.append(m)
    return vals, rank


def _route_kernel(h_ref, wq_ref, sk_ref, r2_ref, e2_ref, c1_ref, e1_ref, cand, selm, *, nheads):
    rt = h_ref.shape[0]
    q_t = _dot_nt(wq_ref[...], h_ref[...])
    cand[_CAND_ROWS - 8:, :] = jnp.full((8, rt), NEG_INF, F32)
    iota_c = lax.broadcasted_iota(jnp.int32, (_CAND_ROWS, rt), 0).astype(F32)
    for h in range(nheads):
        qa = q_t[(2 * h) * N_KEYS:(2 * h + 1) * N_KEYS].astype(BF16)
        qb = q_t[(2 * h + 1) * N_KEYS:(2 * h + 2) * N_KEYS].astype(BF16)
        s1 = _dot(sk_ref[2 * h], qa)
        s2 = _dot(sk_ref[2 * h + 1], qb)
        v1, rank1 = _top16(s1, N_KEYS)
        v2, rank2 = _top16(s2, N_KEYS)
        for row, (i, j) in enumerate(_PAIRS):
            cand[row:row + 1, :] = v1[i] + v2[j]
        cur = cand[...]
        top = cur[0:1, :]
        sel_all = jnp.zeros(cur.shape, F32)
        for _ in range(TOPK):
            m = jnp.max(cur, axis=0, keepdims=True)
            idx = jnp.min(jnp.where(cur == m, iota_c, float(_CAND_ROWS)), axis=0, keepdims=True)
            sel = iota_c == idx
            sel_all = jnp.where(sel, 1.0, sel_all)
            cur = jnp.where(sel, NEG_INF, cur)
        z = jnp.sum(sel_all * jnp.exp(cand[...] - top), axis=0, keepdims=True)
        selm[...] = sel_all
        c1 = jnp.zeros(rank1.shape, F32)
        row = 0
        for i in range(TOPK):
            n_i = TOPK // (i + 1)
            cnt = jnp.sum(selm[row:row + n_i, :], axis=0, keepdims=True)
            c1 = jnp.where(rank1 == float(i), cnt, c1)
            row += n_i
        e1 = jnp.where(rank1 < float(TOPK), jnp.exp(s1 - v1[0]), 0.0) * (1.0 / z)
        e2 = jnp.where(rank2 < float(TOPK), jnp.exp(s2 - v2[0]), 0.0)
        r2_ref[h] = rank2.astype(BF16)
        e2_ref[h] = e2.astype(BF16)
        c1_ref[h] = c1
        e1_ref[h] = e1


def _peer_route(h2b, wq_t, sk, rt):
    n, d = h2b.shape
    nheads = sk.shape[0] // 2
    tab = pl.BlockSpec((nheads, N_KEYS, rt), lambda i: (0, 0, i))
    full = lambda a: pl.BlockSpec(a.shape, lambda i: (0,) * a.ndim)
    return pl.pallas_call(
        functools.partial(_route_kernel, nheads=nheads),
        grid=(n // rt,),
        in_specs=[pl.BlockSpec((rt, d), lambda i: (i, 0)), full(wq_t), full(sk)],
        out_specs=[tab] * 4,
        out_shape=[jax.ShapeDtypeStruct((nheads, N_KEYS, n), BF16)] * 2
        + [jax.ShapeDtypeStruct((nheads, N_KEYS, n), F32)] * 2,
        scratch_shapes=[pltpu.VMEM((_CAND_ROWS, rt), F32), pltpu.VMEM((_CAND_ROWS, rt), F32)],
        compiler_params=_params(("arbitrary",)),
        name="peer_route",
    )(h2b, wq_t, sk)


def _gelu_tanh(x):
    return 0.5 * x * (1.0 + jnp.tanh(math.sqrt(2.0 / math.pi) * (x + 0.044715 * (x * x * x))))


def _peer_kernel(h_ref, u_ref, vt_ref, r2_ref, e2_ref, c1_ref, e1_ref, o_ref, *, a_per, nheads):
    j = pl.program_id(1)

    @pl.when(j == 0)
    def _():
        o_ref[...] = jnp.zeros(o_ref.shape, F32)

    hb = h_ref[...]
    sub = 2 * N_KEYS
    acc = None
    for sb in range(a_per // 2):
        h_t = _dot_nt(u_ref[sb * sub:(sb + 1) * sub, :], hb)
        parts = []
        for a2 in range(2):
            a_glob = j * a_per + sb * 2 + a2
            act = _gelu_tanh(h_t[a2 * N_KEYS:(a2 + 1) * N_KEYS]).astype(BF16)
            gate = None
            for h in range(nheads):
                cnt = c1_ref[h, pl.ds(a_glob, 1), :].astype(BF16)
                w1 = e1_ref[h, pl.ds(a_glob, 1), :].astype(BF16)
                term = jnp.where(r2_ref[h] < cnt, e2_ref[h], jnp.zeros((), BF16)) * w1
                gate = term if gate is None else gate + term
            parts.append(gate * act)
        contrib = _dot(vt_ref[:, sb * sub:(sb + 1) * sub], jnp.concatenate(parts, axis=0))
        acc = contrib if acc is None else acc + contrib
    o_ref[...] += acc


def _peer_dense(h2b, u_b, v_t, r2, e2, c1, e1, rt, a_per):
    n, d = h2b.shape
    n_exp = u_b.shape[0]
    nheads = r2.shape[0]
    et = a_per * N_KEYS
    tab = pl.BlockSpec((nheads, N_KEYS, rt), lambda i, j: (0, 0, i))
    return pl.pallas_call(
        functools.partial(_peer_kernel, a_per=a_per, nheads=nheads),
        grid=(n // rt, n_exp // et),
        in_specs=[pl.BlockSpec((rt, d), lambda i, j: (i, 0)),
                  pl.BlockSpec((et, d), lambda i, j: (j, 0)),
                  pl.BlockSpec((d, et), lambda i, j: (0, j)),
                  tab, tab, tab, tab],
        out_specs=pl.BlockSpec((d, rt), lambda i, j: (0, i)),
        out_shape=jax.ShapeDtypeStruct((d, n), F32),
        compiler_params=_params(("arbitrary", "arbitrary")),
        name="peer_dense",
    )(h2b, u_b, v_t, r2, e2, c1, e1)


def _fin_kernel(x_ref, f_ref, gt_ref, g_ref, o_ref):
    o_ref[...] = _rms(x_ref[...] + gt_ref[0] * f_ref[...], g_ref[...])


def _final(x1, ffn, gt, gf, rt, rows_per_group):
    n, d = x1.shape
    row = pl.BlockSpec((rt, d), lambda i: (i, 0))
    return pl.pallas_call(
        _fin_kernel,
        grid=(n // rt,),
        in_specs=[row, row, _mod_spec(gt, rt, rows_per_group), pl.BlockSpec(gf.shape, lambda i: (0, 0))],
        out_specs=row,
        out_shape=jax.ShapeDtypeStruct((n, d), F32),
        compiler_params=_params(("arbitrary",)),
        name="final_norm",
    )(x1, ffn, gt, gf)


def _layer(x, mods, per_row, fox_fn, shift_prev, s0, lw, t, rt, peer_rt):
    n, d = x.shape
    nb = n // t
    rows_per_group = n if per_row else t
    sh1, sc1, gt1, sh2, sc2, gt2 = mods
    p, q, k, v, lf = _in_proj(x, lw["g1"], sc1, sh1, lw["wr"], lw["wqkv"], lw["wf"], lw["bf"], rt,
                              rows_per_group)
    rc = p.shape[1]
    p3 = p.reshape(nb, t, rc)
    pprev = jnp.concatenate([shift_prev[:, None, :], p3[:, :-1]], axis=1).reshape(n, rc)
    r, w, k2, vv, nkk, kka, g, bonus = _rwkv_prep(p, pprev, lw["vecs"], lw["w2p"], lw["a2p"], lw["g2"],
                                                  lw["seg"], rt)
    groups = nb // 8
    tc = min(64, t)
    rows = [_to_scan_rows(a, groups, t) for a in (r, w, k2, nkk, kka)]
    y, s_new = _rwkv_scan(*rows, _to_scan_cols(vv, groups, t), _state_to_scan(s0, groups), tc)
    y = _y_from_scan(y, groups, t)
    s_new = _state_from_scan(s_new, groups)
    fox = fox_fn(q, k, v, lf)
    x1, h2b = _out_proj(y, bonus, g, fox, x, gt1, sc2, sh2, lw["gnw"], lw["gnb"], lw["seg"], lw["wo"],
                        lw["g2n"], rt, rows_per_group)
    r2, e2, c1, e1 = _peer_route(h2b, lw["wq_t"], lw["sk"], peer_rt)
    ffn_t = _peer_dense(h2b, lw["u_b"], lw["v_t"], r2, e2, c1, e1, peer_rt, PEER_A_PER_STEP)
    return x1, ffn_t.T, gt2, k, v, lf, s_new, p3[:, -1]


def kernel(x_prompt, x_sample, cache_k, cache_v, cache_logf, state_rwkv, state_shift, page_table, c_prompt, c_sample, w_ada, b_ada, norm1_g, norm2_g, w_in, rwkv_mu, rwkv_w0, rwkv_w2, rwkv_a0, rwkv_a2, rwkv_g2, rwkv_kk, rwkv_ka, rwkv_rk, rwkv_gn_w, rwkv_gn_b, fox_bf, w_out, peer_wq, peer_subkeys, peer_u, peer_v, normf_g):
    depth = w_in.shape[0]
    b, t, d = x_prompt.shape
    bd, td, _ = x_sample.shape
    assert td == 1 and depth == 1
    rw = rwkv_w0.shape[1]
    lora_w, lora_a = rwkv_w2.shape[1], rwkv_a2.shape[1]
    assert lora_w + lora_a == LANES
    rc = rwkv_mu.shape[1]
    nh = fox_bf.shape[1]
    fw = nh * HEAD_DIM
    n_pool, page = cache_k.shape[1], cache_k.shape[2]

    xp = x_prompt.reshape(b * t, d)
    xs = x_sample.reshape(bd, d)
    seg = (jnp.arange(rw)[:, None] // HEAD_DIM == jnp.arange(rw)[None, :] // HEAD_DIM).astype(BF16)
    new_p, new_s = [], []
    for layer in range(depth):
        wl = w_in[layer]
        lw = {
            "g1": norm1_g[layer].reshape(1, d), "g2n": norm2_g[layer].reshape(1, d),
            "wr": wl[:, :rc].astype(BF16), "wqkv": wl[:, rc:rc + 3 * fw].astype(BF16),
            "wf": jnp.pad(wl[:, rc + 3 * fw:], ((0, 0), (0, LANES - nh))).astype(BF16),
            "bf": fox_bf[layer].reshape(1, nh),
            "vecs": [rwkv_mu[layer].reshape(1, rc), rwkv_w0[layer].reshape(1, rw), rwkv_a0[layer].reshape(1, rw),
                     rwkv_kk[layer].reshape(1, rw), rwkv_ka[layer].reshape(1, rw), rwkv_rk[layer].reshape(1, rw)],
            "w2p": jnp.pad(rwkv_w2[layer], ((0, lora_a), (0, 0))).astype(BF16),
            "a2p": jnp.pad(rwkv_a2[layer], ((lora_w, 0), (0, 0))).astype(BF16),
            "g2": rwkv_g2[layer].astype(BF16), "seg": seg,
            "gnw": rwkv_gn_w[layer].reshape(1, rw), "gnb": rwkv_gn_b[layer].reshape(1, rw),
            "wo": w_out[layer].astype(BF16),
            "wq_t": peer_wq[layer].T.astype(BF16),
            "sk": peer_subkeys[layer].reshape(-1, N_KEYS, peer_subkeys.shape[-1]).astype(BF16),
            "u_b": peer_u[layer].astype(BF16), "v_t": peer_v[layer].T.astype(BF16),
        }
        mod = _adaln(jnp.concatenate([c_prompt, c_sample], axis=0), w_ada[layer], b_ada[layer])
        mods_p = [mod[:b, i * d:(i + 1) * d].reshape(b, 1, d) for i in range(6)]
        mods_s = [mod[b:, i * d:(i + 1) * d].reshape(1, bd, d) for i in range(6)]

        def fox_p(q, k, v, lf):
            tq = min(256, t)
            heads = lambda a: a.reshape(b, t, nh, HEAD_DIM).transpose(0, 2, 1, 3)
            v_t = v.reshape(b, t // tq, tq, nh, HEAD_DIM).transpose(0, 3, 1, 4, 2)
            c_t = _cumsum_t(lf.reshape(b, t, nh).transpose(0, 2, 1))
            o = _fox_prompt(heads(q), heads(k), v_t, c_t, tq, ATTN_HEADS_PER_STEP)
            return o.transpose(0, 3, 1, 2).reshape(b * t, fw)

        def fox_s(q, k, v, lf, layer=layer):
            heads = lambda a: a.reshape(bd, nh, HEAD_DIM)
            suf_tot = _page_suffix(cache_logf[layer].transpose(0, 2, 1).reshape(n_pool * nh, page))
            o = _fox_decode(heads(q), heads(k), heads(v), lf,
                            cache_k[layer].transpose(0, 2, 3, 1), cache_v[layer].transpose(0, 2, 3, 1),
                            suf_tot.reshape(n_pool, nh, 1, 2 * page), page_table)
            return o.reshape(bd, fw)

        rt_p = min(512, t)
        x1p, ffn_p, gt2p, k_p, v_p, lf_p, st_p, sh_p = _layer(
            xp, mods_p, False, fox_p, jnp.zeros((b, rc), F32), jnp.zeros((b, rw // HEAD_DIM, HEAD_DIM, HEAD_DIM), F32),
            lw, t, rt_p, min(256, t))
        x1s, ffn_s, gt2s, k_s, v_s, lf_s, st_s, sh_s = _layer(
            xs, mods_s, True, fox_s, state_shift[layer], state_rwkv[layer], lw, 1, bd, bd)
        gf = normf_g.reshape(1, d)
        xp = _final(x1p, ffn_p, gt2p, gf, rt_p, t)
        xs = _final(x1s, ffn_s, gt2s, gf, bd, bd)
        new_p.append((k_p.reshape(b, t, nh, HEAD_DIM), v_p.reshape(b, t, nh, HEAD_DIM), lf_p.reshape(b, t, nh),
                      st_p, sh_p))
        new_s.append((k_s.reshape(bd, 1, nh, HEAD_DIM), v_s.reshape(bd, 1, nh, HEAD_DIM), lf_s.reshape(bd, 1, nh),
                      st_s, sh_s))

    stack = lambda lst, i: jnp.stack([e[i] for e in lst], axis=0)
    return (xp.reshape(b, t, d), xs.reshape(bd, 1, d),
            stack(new_p, 0), stack(new_p, 1), stack(new_p, 2), stack(new_p, 3), stack(new_p, 4),
            stack(new_s, 0), stack(new_s, 1), stack(new_s, 2), stack(new_s, 3), stack(new_s, 4))
```

```python
import functools
import math

import jax
import jax.numpy as jnp
from jax import lax
from jax.experimental import pallas as pl
from jax.experimental.pallas import tpu as pltpu

F32 = jnp.float32
BF16 = jnp.bfloat16

HEAD_DIM = 64
LANES = 128
PAGES_PER_STEP = 8
PEER_A_PER_STEP = 16
ATTN_HEADS_PER_STEP = 4
TOPK = 16
N_KEYS = 128
RMS_EPS = 1e-6
GN_EPS = 64e-5
NEG_INF = float("-inf")
VMEM_LIMIT = 56 * 1024 * 1024

_PAIRS = [(i, j) for i in range(TOPK) for j in range(TOPK) if (i + 1) * (j + 1) <= TOPK]
_N_CAND = len(_PAIRS)
_CAND_ROWS = -(-_N_CAND // 8) * 8


def _params(sem, vmem=VMEM_LIMIT):
    return pltpu.CompilerParams(dimension_semantics=sem, vmem_limit_bytes=vmem)


def _dot(a, b):
    return jnp.dot(a, b, preferred_element_type=F32)


def _dot_nt(a, b):
    return lax.dot_general(a, b, (((1,), (1,)), ((), ())), preferred_element_type=F32)


def _dot_tn(a, b):
    return lax.dot_general(a, b, (((0,), (0,)), ((), ())), preferred_element_type=F32)


def _split3(x):
    hi = x.astype(BF16)
    r1 = x - hi.astype(F32)
    mid = r1.astype(BF16)
    lo = (r1 - mid.astype(F32)).astype(BF16)
    return hi, mid, lo


def _dot_exact_rhs(x, m):
    hi, mid, lo = _split3(x)
    return _dot(hi, m) + _dot(mid, m) + _dot(lo, m)


def _softplus(z):
    return jnp.maximum(z, 0.0) + jnp.log1p(jnp.exp(-jnp.abs(z)))


def _sigmoid(z):
    return 1.0 / (1.0 + jnp.exp(-z))


def _rms(x, g):
    ms = jnp.mean(x * x, axis=-1, keepdims=True)
    return x * lax.rsqrt(ms + RMS_EPS) * g


def _ada_kernel(c_ref, w_ref, b_ref, o_ref):
    c = c_ref[...]
    s = c * _sigmoid(c)
    o_ref[...] = _dot(s.astype(BF16), w_ref[...].astype(BF16)) + b_ref[...]


def _adaln(c, w_ada, b_ada):
    n, d = c.shape
    cols = w_ada.shape[1]
    tn = 1024
    return pl.pallas_call(
        _ada_kernel,
        grid=(cols // tn,),
        in_specs=[pl.BlockSpec((n, d), lambda j: (0, 0)),
                  pl.BlockSpec((d, tn), lambda j: (0, j)),
                  pl.BlockSpec((1, tn), lambda j: (0, j))],
        out_specs=pl.BlockSpec((n, tn), lambda j: (0, j)),
        out_shape=jax.ShapeDtypeStruct((n, cols), F32),
        compiler_params=_params(("arbitrary",)),
        name="adaln",
    )(c, w_ada, b_ada.reshape(1, cols))


def _in_kernel(x_ref, g_ref, sc_ref, sh_ref, wr_ref, wqkv_ref, wf_ref, bf_ref,
               p_ref, q_ref, k_ref, v_ref, lf_ref, *, fw):
    h = _rms(x_ref[...], g_ref[...]) * (1.0 + sc_ref[0]) + sh_ref[0]
    hb = h.astype(BF16)
    p_ref[...] = _dot(hb, wr_ref[...])
    qkv = _dot(hb, wqkv_ref[...])
    q_ref[...] = qkv[:, :fw]
    k_ref[...] = qkv[:, fw:2 * fw]
    v_ref[...] = qkv[:, 2 * fw:]
    f = _dot(hb, wf_ref[...])[:, :bf_ref.shape[1]] + bf_ref[...]
    lf_ref[...] = -_softplus(-f)


def _mod_spec(mod, rt, rows_per_group):
    mr, d = mod.shape[1], mod.shape[2]
    tpg = rows_per_group // rt
    return pl.BlockSpec((1, mr, d), lambda i: (i // tpg, 0, 0))


def _in_proj(x, g1, sc, sh, wr, wqkv, wf, bfox, rt, rows_per_group):
    n, d = x.shape
    rc, fw3, nh = wr.shape[1], wqkv.shape[1], bfox.shape[1]
    fw = fw3 // 3
    row = lambda w: pl.BlockSpec((rt, w), lambda i: (i, 0))
    full = lambda a: pl.BlockSpec(a.shape, lambda i: (0,) * a.ndim)
    return pl.pallas_call(
        functools.partial(_in_kernel, fw=fw),
        grid=(n // rt,),
        in_specs=[row(d), full(g1), _mod_spec(sc, rt, rows_per_group), _mod_spec(sh, rt, rows_per_group),
                  full(wr), full(wqkv), full(wf), full(bfox)],
        out_specs=[row(rc), row(fw), row(fw), row(fw), row(nh)],
        out_shape=[jax.ShapeDtypeStruct((n, rc), F32), jax.ShapeDtypeStruct((n, fw), F32),
                   jax.ShapeDtypeStruct((n, fw), F32), jax.ShapeDtypeStruct((n, fw), F32),
                   jax.ShapeDtypeStruct((n, nh), F32)],
        compiler_params=_params(("arbitrary",)),
        name="in_proj",
    )(x, g1, sc, sh, wr, wqkv, wf, bfox)


def _prep_kernel(p_ref, pp_ref, mu_ref, w0_ref, a0_ref, kkp_ref, ka_ref, rk_ref,
                 w2_ref, a2_ref, g2_ref, seg_ref,
                 r_ref, w_ref, k_ref, v_ref, nkk_ref, kka_ref, g_ref, bonus_ref, *, rw):
    p = p_ref[...]
    xs = p + mu_ref[...] * (pp_ref[...] - p)
    r = xs[:, :rw]
    k = xs[:, rw:2 * rw]
    v = xs[:, 2 * rw:3 * rw]
    xwa = xs[:, 3 * rw:3 * rw + LANES]
    xg = xs[:, 3 * rw + LANES:]
    seg = seg_ref[...]
    lw = w0_ref[...] + _dot(jnp.tanh(xwa).astype(BF16), w2_ref[...])
    w_log = -_softplus(-lw) - 0.5
    decay = jnp.exp(-jnp.exp(w_log))
    a = _sigmoid(a0_ref[...] + _dot(xwa.astype(BF16), a2_ref[...]))
    g = _dot(_sigmoid(xg).astype(BF16), g2_ref[...])
    kk = k * kkp_ref[...]
    nrm = jnp.sqrt(_dot_exact_rhs(kk * kk, seg))
    kkn = kk / jnp.maximum(nrm, 1e-12)
    k2 = k * (1.0 + (a - 1.0) * ka_ref[...])
    bonus = _dot_exact_rhs(r * k2 * rk_ref[...], seg) * v
    r_ref[...] = r
    w_ref[...] = decay
    k_ref[...] = k2
    v_ref[...] = v
    nkk_ref[...] = -kkn
    kka_ref[...] = kkn * a
    g_ref[...] = g
    bonus_ref[...] = bonus


def _rwkv_prep(p, pprev, vecs, w2p, a2p, g2, seg, rt):
    n, rc = p.shape
    rw = seg.shape[0]
    row = lambda w: pl.BlockSpec((rt, w), lambda i: (i, 0))
    full = lambda a: pl.BlockSpec(a.shape, lambda i: (0,) * a.ndim)
    return pl.pallas_call(
        functools.partial(_prep_kernel, rw=rw),
        grid=(n // rt,),
        in_specs=[row(rc), row(rc)] + [full(a) for a in vecs] + [full(w2p), full(a2p), full(g2), full(seg)],
        out_specs=[row(rw)] * 8,
        out_shape=[jax.ShapeDtypeStruct((n, rw), F32)] * 8,
        compiler_params=_params(("arbitrary",)),
        name="rwkv_prep",
    )(p, pprev, *vecs, w2p, a2p, g2, seg)


def _scan_kernel(r_ref, w_ref, k_ref, nkk_ref, kka_ref, v_ref, s0_ref, y_ref, sT_ref, S, *, tc, nk2):
    tb = pl.program_id(1)

    @pl.when(tb == 0)
    def _():
        S[...] = s0_ref[0]

    def step(t, carry):
        vv = v_ref[0, t]
        sa = jnp.zeros(vv.shape, F32)
        for k2 in range(nk2):
            sa = sa + S[k2] * nkk_ref[0, t, k2:k2 + 1, :]
        sa = sa + pltpu.roll(sa, 64, axis=1)
        y = jnp.zeros(vv.shape, F32)
        for k2 in range(nk2):
            s = (S[k2] * w_ref[0, t, k2:k2 + 1, :] + sa * kka_ref[0, t, k2:k2 + 1, :]
                 + vv * k_ref[0, t, k2:k2 + 1, :])
            S[k2] = s
            y = y + s * r_ref[0, t, k2:k2 + 1, :]
        y = y + pltpu.roll(y, 64, axis=1)
        y_ref[0, t] = y[:, :64]
        return carry

    lax.fori_loop(0, tc, step, 0)

    @pl.when(tb == pl.num_programs(1) - 1)
    def _():
        sT_ref[0] = S[...]


def _rwkv_scan(r, w, k, nkk, kka, v2, s0, tc):
    g, t, nk2, ln = r.shape
    nv = v2.shape[2]
    rowspec = pl.BlockSpec((1, tc, nk2, ln), lambda gi, ti: (gi, ti, 0, 0))
    sspec = pl.BlockSpec((1, nk2, nv, ln), lambda gi, ti: (gi, 0, 0, 0))
    return pl.pallas_call(
        functools.partial(_scan_kernel, tc=tc, nk2=nk2),
        grid=(g, t // tc),
        in_specs=[rowspec] * 5 + [pl.BlockSpec((1, tc, nv, ln), lambda gi, ti: (gi, ti, 0, 0)), sspec],
        out_specs=[pl.BlockSpec((1, tc, nv, 64), lambda gi, ti: (gi, ti, 0, 0)), sspec],
        out_shape=[jax.ShapeDtypeStruct((g, t, nv, 64), F32), jax.ShapeDtypeStruct(s0.shape, F32)],
        scratch_shapes=[pltpu.VMEM((nk2, nv, ln), F32)],
        compiler_params=_params(("arbitrary", "arbitrary")),
        name="rwkv_scan",
    )(r, w, k, nkk, kka, v2, s0)


def _to_scan_rows(a, groups, t):
    a = a.reshape(groups, 8, t, 8, 64).transpose(0, 2, 4, 1, 3)
    return a.reshape(groups, t, 32, 128)


def _to_scan_cols(a, groups, t):
    a = a.reshape(groups, 8, t, 8, 64).transpose(0, 2, 4, 1, 3).reshape(groups, t, 64, 64)
    return jnp.concatenate([a, a], axis=-1)


def _state_to_scan(s, groups):
    s = s.reshape(groups, 8, 8, 64, 32, 2).transpose(0, 4, 3, 5, 1, 2)
    return s.reshape(groups, 32, 64, 128)


def _state_from_scan(s, groups):
    s = s.reshape(groups, 32, 64, 2, 8, 8).transpose(0, 4, 5, 2, 1, 3)
    return s.reshape(groups * 8, 8, 64, 64)


def _y_from_scan(y, groups, t):
    y = y.reshape(groups, t, 64, 8, 8).transpose(0, 3, 1, 4, 2)
    return y.reshape(groups * 8 * t, 8 * 64)


def _cum_kernel(lf_ref, c_ref, *, blk):
    t = lf_ref.shape[2]
    src = lax.broadcasted_iota(jnp.int32, (blk, blk), 0)
    dst = lax.broadcasted_iota(jnp.int32, (blk, blk), 1)
    tri = (src <= dst).astype(BF16)
    carry = jnp.zeros((lf_ref.shape[1], 1), F32)
    for i in range(t // blk):
        cs = _dot_exact_rhs(lf_ref[0, :, i * blk:(i + 1) * blk], tri) + carry
        c_ref[0, :, i * blk:(i + 1) * blk] = cs
        carry = cs[:, blk - 1:blk]


def _cumsum_t(lf_t):
    b, h, t = lf_t.shape
    blk = min(256, t)
    spec = pl.BlockSpec((1, h, t), lambda i: (i, 0, 0))
    return pl.pallas_call(
        functools.partial(_cum_kernel, blk=blk),
        grid=(b,), in_specs=[spec], out_specs=spec,
        out_shape=jax.ShapeDtypeStruct((b, h, t), F32),
        compiler_params=_params(("arbitrary",)),
        name="logf_cumsum",
    )(lf_t)


def _attn_kernel(q_ref, k_ref, vt_ref, cc_ref, cr_ref, o_ref, *, tq, hb):
    qi = pl.program_id(2)
    qs = [(q_ref[0, g] * (HEAD_DIM ** -0.5)).astype(BF16) for g in range(hb)]
    cis = [cr_ref[0, g, qi] for g in range(hb)]

    def block(j, carry, masked):
        start = pl.multiple_of(j * tq, tq)
        scores = [_dot_nt(k_ref[0, g, pl.ds(start, tq), :].astype(BF16), qs[g]) for g in range(hb)]
        stats = []
        for g in range(hb):
            m, l, _ = carry[g]
            s = scores[g] + (cis[g] - cc_ref[0, g, j])
            if masked:
                key = lax.broadcasted_iota(jnp.int32, (tq, tq), 0)
                qry = lax.broadcasted_iota(jnp.int32, (tq, tq), 1)
                s = jnp.where(key <= qry, s, NEG_INF)
            m_new = jnp.maximum(m, jnp.max(s, axis=0, keepdims=True))
            alpha = jnp.exp(m - m_new)
            p = jnp.exp(s - m_new)
            stats.append((m_new, l * alpha + jnp.sum(p, axis=0, keepdims=True), alpha, p.astype(BF16)))
        out = []
        for g in range(hb):
            m_new, l, alpha, p = stats[g]
            vj = vt_ref[0, g, pl.ds(start, tq), :].astype(BF16)
            acc = carry[g][2] * alpha + _dot_tn(vj, p)
            out.append((m_new, l, acc))
        return tuple(out)

    init = tuple((jnp.full((1, tq), NEG_INF, F32), jnp.zeros((1, tq), F32), jnp.zeros((HEAD_DIM, tq), F32))
                 for _ in range(hb))
    carry = lax.fori_loop(0, qi, lambda j, c: block(j, c, False), init)
    carry = block(qi, carry, True)
    for g in range(hb):
        o_ref[0, g] = carry[g][2] / carry[g][1]


def _fox_prompt(q, k, v_t, c_t, tq, hb):
    b, h, t, d = q.shape
    nq = t // tq
    c_col = c_t.reshape(b, h, nq, tq, 1)
    c_row = c_t.reshape(b, h, nq, 1, tq)
    whole = lambda shape: pl.BlockSpec((1, hb) + shape, lambda bi, hi, i: (bi, hi) + (0,) * len(shape))
    return pl.pallas_call(
        functools.partial(_attn_kernel, tq=tq, hb=hb),
        grid=(b, h // hb, nq),
        in_specs=[pl.BlockSpec((1, hb, tq, d), lambda bi, hi, i: (bi, hi, i, 0)),
                  whole((t, d)), whole((t, d)), whole((nq, tq, 1)), whole((nq, 1, tq))],
        out_specs=pl.BlockSpec((1, hb, d, tq), lambda bi, hi, i: (bi, hi, 0, i)),
        out_shape=jax.ShapeDtypeStruct((b, h, d, t), F32),
        compiler_params=_params(("arbitrary", "arbitrary", "arbitrary")),
        name="fox_prompt",
    )(q, k, v_t, c_col, c_row)


def _suffix_kernel(lf_ref, o_ref):
    page = lf_ref.shape[1]
    src = lax.broadcasted_iota(jnp.int32, (page, page), 0)
    dst = lax.broadcasted_iota(jnp.int32, (page, page), 1)
    hi, mid, lo = _split3(lf_ref[...])
    later = (src > dst).astype(BF16)
    ones = jnp.ones((page, page), BF16)
    o_ref[:, :page] = _dot(hi, later) + _dot(mid, later) + _dot(lo, later)
    o_ref[:, page:] = _dot(hi, ones) + _dot(mid, ones) + _dot(lo, ones)


def _page_suffix(lf_rows):
    n, page = lf_rows.shape
    rt = math.gcd(n, 2048)
    return pl.pallas_call(
        _suffix_kernel,
        grid=(n // rt,),
        in_specs=[pl.BlockSpec((rt, page), lambda i: (i, 0))],
        out_specs=pl.BlockSpec((rt, 2 * page), lambda i: (i, 0)),
        out_shape=jax.ShapeDtypeStruct((n, 2 * page), F32),
        compiler_params=_params(("arbitrary",)),
        name="logf_page_suffix",
    )(lf_rows)


def _dec_kernel(pt_ref, q_ref, kn_ref, vn_ref, lfn_ref, *refs, npg):
    k_refs = refs[:npg]
    v_refs = refs[npg:2 * npg]
    st_refs = refs[2 * npg:3 * npg]
    o_ref = refs[3 * npg]
    m_s, l_s, acc_s, car_s = refs[3 * npg + 1:]
    s_id = pl.program_id(1)
    page = k_refs[0].shape[3]
    q = q_ref[0] * (HEAD_DIM ** -0.5)

    @pl.when(s_id == 0)
    def _():
        m_s[...] = jnp.sum(q * kn_ref[0], axis=1, keepdims=True)[:, :, :1]
        l_s[...] = jnp.ones(l_s.shape, F32)
        acc_s[...] = vn_ref[0]
        car_s[...] = jnp.zeros(car_s.shape, F32)

    carry = car_s[...]
    base = lfn_ref[0]
    logits = []
    for i in range(npg):
        qk = jnp.sum(q * k_refs[i][0], axis=1, keepdims=True)
        logits.append(qk + (st_refs[i][0, :, :, :page] + (carry + base)))
        carry = carry + st_refs[i][0, :, :, page:]
    car_s[...] = carry
    m_old = m_s[...]
    m_new = m_old
    for s in logits:
        m_new = jnp.maximum(m_new, jnp.max(s, axis=2, keepdims=True))
    alpha = jnp.exp(m_old - m_new)
    l_new = l_s[...] * alpha
    acc = acc_s[...] * alpha
    for i in range(npg):
        p = jnp.exp(logits[i] - m_new)
        l_new = l_new + jnp.sum(p, axis=2, keepdims=True)
        acc = acc + v_refs[i][0] * p
    l_s[...] = l_new
    m_s[...] = m_new
    acc_s[...] = acc

    @pl.when(s_id == pl.num_programs(1) - 1)
    def _():
        o_ref[0] = jnp.sum(acc, axis=2, keepdims=True) / l_new


def _fox_decode(q, k_new, v_new, lf_new, cache_k, cache_v, suf_tot, page_table):
    bd, nh, hd = q.shape
    n_pages = page_table.shape[1]
    page = cache_k.shape[3]
    npg = PAGES_PER_STEP
    assert n_pages % npg == 0 and page == LANES
    steps = n_pages // npg

    def page_idx(i):
        return lambda b, s, pt: (pt[b, n_pages - 1 - (s * npg + i)], 0, 0, 0)

    on_lanes = lambda a: jnp.broadcast_to(a[..., None], a.shape + (LANES,))
    lane0 = lambda a: jnp.pad(a[..., None], ((0, 0),) * a.ndim + ((0, LANES - 1),))
    featspec = pl.BlockSpec((1, nh, hd, LANES), lambda b, s, pt: (b, 0, 0, 0))
    headspec = pl.BlockSpec((1, nh, 1, LANES), lambda b, s, pt: (b, 0, 0, 0))
    in_specs = [featspec, featspec, featspec, headspec]
    in_specs += [pl.BlockSpec((1, nh, hd, page), page_idx(i)) for i in range(npg)]
    in_specs += [pl.BlockSpec((1, nh, hd, page), page_idx(i)) for i in range(npg)]
    in_specs += [pl.BlockSpec((1, nh, 1, 2 * page), page_idx(i)) for i in range(npg)]
    grid_spec = pltpu.PrefetchScalarGridSpec(
        num_scalar_prefetch=1, grid=(bd, steps), in_specs=in_specs,
        out_specs=pl.BlockSpec((1, nh, hd, 1), lambda b, s, pt: (b, 0, 0, 0)),
        scratch_shapes=[pltpu.VMEM((nh, 1, 1), F32), pltpu.VMEM((nh, 1, 1), F32),
                        pltpu.VMEM((nh, hd, LANES), F32), pltpu.VMEM((nh, 1, LANES), F32)])
    return pl.pallas_call(
        functools.partial(_dec_kernel, npg=npg),
        grid_spec=grid_spec,
        out_shape=jax.ShapeDtypeStruct((bd, nh, hd, 1), F32),
        compiler_params=_params(("arbitrary", "arbitrary")),
        name="fox_decode",
    )(page_table, on_lanes(q), on_lanes(k_new), lane0(v_new), on_lanes(lf_new)[:, :, None, :],
      *([cache_k] * npg), *([cache_v] * npg), *([suf_tot] * npg))


def _out_kernel(y_ref, bonus_ref, g_ref, fox_ref, x_ref, gt_ref, sc_ref, sh_ref,
                gnw_ref, gnb_ref, seg_ref, wo_ref, g2_ref, x1_ref, h2b_ref, *, rw):
    seg = seg_ref[...]
    y = y_ref[...]
    mean = _dot_exact_rhs(y, seg) * (1.0 / HEAD_DIM)
    d = y - mean
    var = _dot_exact_rhs(d * d, seg) * (1.0 / HEAD_DIM)
    yn = d * lax.rsqrt(var + GN_EPS) * gnw_ref[...] + gnb_ref[...]
    ro = (yn + bonus_ref[...]) * g_ref[...]
    mixed = _dot(ro.astype(BF16), wo_ref[:rw, :]) + _dot(fox_ref[...].astype(BF16), wo_ref[rw:, :])
    x1 = x_ref[...] + gt_ref[0] * mixed
    h2 = _rms(x1, g2_ref[...]) * (1.0 + sc_ref[0]) + sh_ref[0]
    x1_ref[...] = x1
    h2b_ref[...] = h2.astype(BF16)


def _out_proj(y, bonus, g, fox, x, gt, sc, sh, gnw, gnb, seg, wo, g2, rt, rows_per_group):
    n, d = x.shape
    rw = y.shape[1]
    row = lambda w: pl.BlockSpec((rt, w), lambda i: (i, 0))
    full = lambda a: pl.BlockSpec(a.shape, lambda i: (0,) * a.ndim)
    ms = lambda m: _mod_spec(m, rt, rows_per_group)
    return pl.pallas_call(
        functools.partial(_out_kernel, rw=rw),
        grid=(n // rt,),
        in_specs=[row(rw), row(rw), row(rw), row(fox.shape[1]), row(d), ms(gt), ms(sc), ms(sh),
                  full(gnw), full(gnb), full(seg), full(wo), full(g2)],
        out_specs=[row(d), row(d)],
        out_shape=[jax.ShapeDtypeStruct((n, d), F32), jax.ShapeDtypeStruct((n, d), BF16)],
        compiler_params=_params(("arbitrary",)),
        name="out_proj",
    )(y, bonus, g, fox, x, gt, sc, sh, gnw, gnb, seg, wo, g2)


def _top16(s, rows):
    iota = lax.broadcasted_iota(jnp.int32, s.shape, 0).astype(F32)
    rank = jnp.full(s.shape, float(TOPK), F32)
    cur = s
    vals = []
    for i in range(TOPK):
        m = jnp.max(cur, axis=0, keepdims=True)
        idx = jnp.min(jnp.where(cur == m, iota, float(rows)), axis=0, keepdims=True)
        sel = iota == idx
        rank = jnp.where(sel, float(i), rank)
        cur = jnp.where(sel, NEG_INF, cur)
        vals.append(m)
    return vals, rank


def _route_kernel(h_ref, wq_ref, sk_ref, r2_ref, e2_ref, c1_ref, e1_ref, cand, selm, *, nheads):
    rt = h_ref.shape[0]
    q_t = _dot_nt(wq_ref[...], h_ref[...])
    cand[_CAND_ROWS - 8:, :] = jnp.full((8, rt), NEG_INF, F32)
    iota_c = lax.broadcasted_iota(jnp.int32, (_CAND_ROWS, rt), 0).astype(F32)
    for h in range(nheads):
        qa = q_t[(2 * h) * N_KEYS:(2 * h + 1) * N_KEYS].astype(BF16)
        qb = q_t[(2 * h + 1) * N_KEYS:(2 * h + 2) * N_KEYS].astype(BF16)
        s1 = _dot(sk_ref[2 * h], qa)
        s2 = _dot(sk_ref[2 * h + 1], qb)
        v1, rank1 = _top16(s1, N_KEYS)
        v2, rank2 = _top16(s2, N_KEYS)
        for row, (i, j) in enumerate(_PAIRS):
            cand[row:row + 1, :] = v1[i] + v2[j]
        cur = cand[...]
        top = cur[0:1, :]
        sel_all = jnp.zeros(cur.shape, F32)
        for _ in range(TOPK):
            m = jnp.max(cur, axis=0, keepdims=True)
            idx = jnp.min(jnp.where(cur == m, iota_c, float(_CAND_ROWS)), axis=0, keepdims=True)
            sel = iota_c == idx
            sel_all = jnp.where(sel, 1.0, sel_all)
            cur = jnp.where(sel, NEG_INF, cur)
        z = jnp.sum(sel_all * jnp.exp(cand[...] - top), axis=0, keepdims=True)
        selm[...] = sel_all
        c1 = jnp.zeros(rank1.shape, F32)
        row = 0
        for i in range(TOPK):
            n_i = TOPK // (i + 1)
            cnt = jnp.sum(selm[row:row + n_i, :], axis=0, keepdims=True)
            c1 = jnp.where(rank1 == float(i), cnt, c1)
            row += n_i
        e1 = jnp.where(rank1 < float(TOPK), jnp.exp(s1 - v1[0]), 0.0) * (1.0 / z)
        e2 = jnp.where(rank2 < float(TOPK), jnp.exp(s2 - v2[0]), 0.0)
        r2_ref[h] = rank2.astype(BF16)
        e2_ref[h] = e2.astype(BF16)
        c1_ref[h] = c1
        e1_ref[h] = e1


def _peer_route(h2b, wq_t, sk, rt):
    n, d = h2b.shape
    nheads = sk.shape[0] // 2
    tab = pl.BlockSpec((nheads, N_KEYS, rt), lambda i: (0, 0, i))
    full = lambda a: pl.BlockSpec(a.shape, lambda i: (0,) * a.ndim)
    return pl.pallas_call(
        functools.partial(_route_kernel, nheads=nheads),
        grid=(n // rt,),
        in_specs=[pl.BlockSpec((rt, d), lambda i: (i, 0)), full(wq_t), full(sk)],
        out_specs=[tab] * 4,
        out_shape=[jax.ShapeDtypeStruct((nheads, N_KEYS, n), BF16)] * 2
        + [jax.ShapeDtypeStruct((nheads, N_KEYS, n), F32)] * 2,
        scratch_shapes=[pltpu.VMEM((_CAND_ROWS, rt), F32), pltpu.VMEM((_CAND_ROWS, rt), F32)],
        compiler_params=_params(("arbitrary",)),
        name="peer_route",
    )(h2b, wq_t, sk)


def _gelu_tanh(x):
    return 0.5 * x * (1.0 + jnp.tanh(math.sqrt(2.0 / math.pi) * (x + 0.044715 * (x * x * x))))


def _peer_kernel(h_ref, u_ref, vt_ref, r2_ref, e2_ref, c1_ref, e1_ref, o_ref, *, a_per, nheads):
    j = pl.program_id(1)

    @pl.when(j == 0)
    def _():
        o_ref[...] = jnp.zeros(o_ref.shape, F32)

    hb = h_ref[...]
    sub = 2 * N_KEYS
    acc = None
    for sb in range(a_per // 2):
        h_t = _dot_nt(u_ref[sb * sub:(sb + 1) * sub, :], hb)
        parts = []
        for a2 in range(2):
            a_glob = j * a_per + sb * 2 + a2
            act = _gelu_tanh(h_t[a2 * N_KEYS:(a2 + 1) * N_KEYS]).astype(BF16)
            gate = None
            for h in range(nheads):
                cnt = c1_ref[h, pl.ds(a_glob, 1), :].astype(BF16)
                w1 = e1_ref[h, pl.ds(a_glob, 1), :].astype(BF16)
                term = jnp.where(r2_ref[h] < cnt, e2_ref[h], jnp.zeros((), BF16)) * w1
                gate = term if gate is None else gate + term
            parts.append(gate * act)
        contrib = _dot(vt_ref[:, sb * sub:(sb + 1) * sub], jnp.concatenate(parts, axis=0))
        acc = contrib if acc is None else acc + contrib
    o_ref[...] += acc


def _peer_dense(h2b, u_b, v_t, r2, e2, c1, e1, rt, a_per):
    n, d = h2b.shape
    n_exp = u_b.shape[0]
    nheads = r2.shape[0]
    et = a_per * N_KEYS
    tab = pl.BlockSpec((nheads, N_KEYS, rt), lambda i, j: (0, 0, i))
    return pl.pallas_call(
        functools.partial(_peer_kernel, a_per=a_per, nheads=nheads),
        grid=(n // rt, n_exp // et),
        in_specs=[pl.BlockSpec((rt, d), lambda i, j: (i, 0)),
                  pl.BlockSpec((et, d), lambda i, j: (j, 0)),
                  pl.BlockSpec((d, et), lambda i, j: (0, j)),
                  tab, tab, tab, tab],
        out_specs=pl.BlockSpec((d, rt), lambda i, j: (0, i)),
        out_shape=jax.ShapeDtypeStruct((d, n), F32),
        compiler_params=_params(("arbitrary", "arbitrary")),
        name="peer_dense",
    )(h2b, u_b, v_t, r2, e2, c1, e1)


def _fin_kernel(x_ref, f_ref, gt_ref, g_ref, o_ref):
    o_ref[...] = _rms(x_ref[...] + gt_ref[0] * f_ref[...], g_ref[...])


def _final(x1, ffn, gt, gf, rt, rows_per_group):
    n, d = x1.shape
    row = pl.BlockSpec((rt, d), lambda i: (i, 0))
    return pl.pallas_call(
        _fin_kernel,
        grid=(n // rt,),
        in_specs=[row, row, _mod_spec(gt, rt, rows_per_group), pl.BlockSpec(gf.shape, lambda i: (0, 0))],
        out_specs=row,
        out_shape=jax.ShapeDtypeStruct((n, d), F32),
        compiler_params=_params(("arbitrary",)),
        name="final_norm",
    )(x1, ffn, gt, gf)


def _layer(x, mods, per_row, fox_fn, shift_prev, s0, lw, t, rt, peer_rt):
    n, d = x.shape
    nb = n // t
    rows_per_group = n if per_row else t
    sh1, sc1, gt1, sh2, sc2, gt2 = mods
    p, q, k, v, lf = _in_proj(x, lw["g1"], sc1, sh1, lw["wr"], lw["wqkv"], lw["wf"], lw["bf"], rt,
                              rows_per_group)
    rc = p.shape[1]
    p3 = p.reshape(nb, t, rc)
    pprev = jnp.concatenate([shift_prev[:, None, :], p3[:, :-1]], axis=1).reshape(n, rc)
    r, w, k2, vv, nkk, kka, g, bonus = _rwkv_prep(p, pprev, lw["vecs"], lw["w2p"], lw["a2p"], lw["g2"],
                                                  lw["seg"], rt)
    groups = nb // 8
    tc = min(64, t)
    rows = [_to_scan_rows(a, groups, t) for a in (r, w, k2, nkk, kka)]
    y, s_new = _rwkv_scan(*rows, _to_scan_cols(vv, groups, t), _state_to_scan(s0, groups), tc)
    y = _y_from_scan(y, groups, t)
    s_new = _state_from_scan(s_new, groups)
    fox = fox_fn(q, k, v, lf)
    x1, h2b = _out_proj(y, bonus, g, fox, x, gt1, sc2, sh2, lw["gnw"], lw["gnb"], lw["seg"], lw["wo"],
                        lw["g2n"], rt, rows_per_group)
    r2, e2, c1, e1 = _peer_route(h2b, lw["wq_t"], lw["sk"], peer_rt)
    ffn_t = _peer_dense(h2b, lw["u_b"], lw["v_t"], r2, e2, c1, e1, peer_rt, PEER_A_PER_STEP)
    return x1, ffn_t.T, gt2, k, v, lf, s_new, p3[:, -1]


def kernel(x_prompt, x_sample, cache_k, cache_v, cache_logf, state_rwkv, state_shift, page_table, c_prompt, c_sample, w_ada, b_ada, norm1_g, norm2_g, w_in, rwkv_mu, rwkv_w0, rwkv_w2, rwkv_a0, rwkv_a2, rwkv_g2, rwkv_kk, rwkv_ka, rwkv_rk, rwkv_gn_w, rwkv_gn_b, fox_bf, w_out, peer_wq, peer_subkeys, peer_u, peer_v, normf_g):
    depth = w_in.shape[0]
    b, t, d = x_prompt.shape
    bd, td, _ = x_sample.shape
    assert td == 1 and depth == 1
    rw = rwkv_w0.shape[1]
    lora_w, lora_a = rwkv_w2.shape[1], rwkv_a2.shape[1]
    assert lora_w + lora_a == LANES
    rc = rwkv_mu.shape[1]
    nh = fox_bf.shape[1]
    fw = nh * HEAD_DIM
    n_pool, page = cache_k.shape[1], cache_k.shape[2]

    xp = x_prompt.reshape(b * t, d)
    xs = x_sample.reshape(bd, d)
    seg = (jnp.arange(rw)[:, None] // HEAD_DIM == jnp.arange(rw)[None, :] // HEAD_DIM).astype(BF16)
    new_p, new_s = [], []
    for layer in range(depth):
        wl = w_in[layer]
        lw = {
            "g1": norm1_g[layer].reshape(1, d), "g2n": norm2_g[layer].reshape(1, d),
            "wr": wl[:, :rc].astype(BF16), "wqkv": wl[:, rc:rc + 3 * fw].astype(BF16),
            "wf": jnp.pad(wl[:, rc + 3 * fw:], ((0, 0), (0, LANES - nh))).astype(BF16),
            "bf": fox_bf[layer].reshape(1, nh),
            "vecs": [rwkv_mu[layer].reshape(1, rc), rwkv_w0[layer].reshape(1, rw), rwkv_a0[layer].reshape(1, rw),
                     rwkv_kk[layer].reshape(1, rw), rwkv_ka[layer].reshape(1, rw), rwkv_rk[layer].reshape(1, rw)],
            "w2p": jnp.pad(rwkv_w2[layer], ((0, lora_a), (0, 0))).astype(BF16),
            "a2p": jnp.pad(rwkv_a2[layer], ((lora_w, 0), (0, 0))).astype(BF16),
            "g2": rwkv_g2[layer].astype(BF16), "seg": seg,
            "gnw": rwkv_gn_w[layer].reshape(1, rw), "gnb": rwkv_gn_b[layer].reshape(1, rw),
            "wo": w_out[layer].astype(BF16),
            "wq_t": peer_wq[layer].T.astype(BF16),
            "sk": peer_subkeys[layer].reshape(-1, N_KEYS, peer_subkeys.shape[-1]).astype(BF16),
            "u_b": peer_u[layer].astype(BF16), "v_t": peer_v[layer].T.astype(BF16),
        }
        mod = _adaln(jnp.concatenate([c_prompt, c_sample], axis=0), w_ada[layer], b_ada[layer])
        mods_p = [mod[:b, i * d:(i + 1) * d].reshape(b, 1, d) for i in range(6)]
        mods_s = [mod[b:, i * d:(i + 1) * d].reshape(1, bd, d) for i in range(6)]

        def fox_p(q, k, v, lf):
            tq = min(256, t)
            heads = lambda a: a.reshape(b, t, nh, HEAD_DIM).transpose(0, 2, 1, 3)
            c_t = _cumsum_t(lf.reshape(b, t, nh).transpose(0, 2, 1))
            o = _fox_prompt(heads(q), heads(k), heads(v), c_t, tq, ATTN_HEADS_PER_STEP)
            return o.transpose(0, 3, 1, 2).reshape(b * t, fw)

        def fox_s(q, k, v, lf, layer=layer):
            heads = lambda a: a.reshape(bd, nh, HEAD_DIM)
            suf_tot = _page_suffix(cache_logf[layer].transpose(0, 2, 1).reshape(n_pool * nh, page))
            o = _fox_decode(heads(q), heads(k), heads(v), lf,
                            cache_k[layer].transpose(0, 2, 3, 1), cache_v[layer].transpose(0, 2, 3, 1),
                            suf_tot.reshape(n_pool, nh, 1, 2 * page), page_table)
            return o.reshape(bd, fw)

        rt_p = min(512, t)
        x1p, ffn_p, gt2p, k_p, v_p, lf_p, st_p, sh_p = _layer(
            xp, mods_p, False, fox_p, jnp.zeros((b, rc), F32), jnp.zeros((b, rw // HEAD_DIM, HEAD_DIM, HEAD_DIM), F32),
            lw, t, rt_p, min(256, t))
        x1s, ffn_s, gt2s, k_s, v_s, lf_s, st_s, sh_s = _layer(
            xs, mods_s, True, fox_s, state_shift[layer], state_rwkv[layer], lw, 1, bd, bd)
        gf = normf_g.reshape(1, d)
        xp = _final(x1p, ffn_p, gt2p, gf, rt_p, t)
        xs = _final(x1s, ffn_s, gt2s, gf, bd, bd)
        new_p.append((k_p.reshape(b, t, nh, HEAD_DIM), v_p.reshape(b, t, nh, HEAD_DIM), lf_p.reshape(b, t, nh),
                      st_p, sh_p))
        new_s.append((k_s.reshape(bd, 1, nh, HEAD_DIM), v_s.reshape(bd, 1, nh, HEAD_DIM), lf_s.reshape(bd, 1, nh),
                      st_s, sh_s))

    stack = lambda lst, i: jnp.stack([e[i] for e in lst], axis=0)
    return (xp.reshape(b, t, d), xs.reshape(bd, 1, d),
            stack(new_p, 0), stack(new_p, 1), stack(new_p, 2), stack(new_p, 3), stack(new_p, 4),
            stack(new_s, 0), stack(new_s, 1), stack(new_s, 2), stack(new_s, 3), stack(new_s, 4))
```
